```python
import math
import jax
import jax.numpy as jnp
from jax import lax
import numpy as np

D_MODEL = 1024
BATCH = 8
SEQ = 4096
DEPTH = 4

GRID_W = 64
CTX_LEN = 256
N_EVEN = (DEPTH + 1) // 2
N_ODD = DEPTH // 2

GDN_HEAD_DIM = 128
GDN_WIDTH = D_MODEL // 2
GDN_HEADS = GDN_WIDTH // GDN_HEAD_DIM
GDN_CHUNK = 64
SHORT_CONV = 3

HYENA_WIDTH = D_MODEL // 2
HYENA_ORDER = 2
HYENA_EMB = 33
HYENA_HIDDEN = 64
HYENA_DECAY_TARGET = 1e-2
HYENA_FAST_DECAY_PCT = 0.3
HYENA_SLOW_DECAY_PCT = 1.5
HYENA_MIN_DECAY = math.log(HYENA_DECAY_TARGET) / HYENA_SLOW_DECAY_PCT
HYENA_MAX_DECAY = math.log(HYENA_DECAY_TARGET) / HYENA_FAST_DECAY_PCT

N_CONV = 3 * GDN_WIDTH + 3 * HYENA_WIDTH
N_IN = N_CONV + GDN_WIDTH + 4 * GDN_HEADS

DIFF_HEAD_DIM = 64
DIFF_HEADS = D_MODEL // (2 * DIFF_HEAD_DIM)
DIFF_V_DIM = 2 * DIFF_HEAD_DIM
DIFF_QK_WIDTH = DIFF_HEADS * 2 * DIFF_HEAD_DIM
DIFF_V_WIDTH = DIFF_HEADS * DIFF_V_DIM
Q_BLOCK = 128
ROPE_BASE = 10000.0
ROPE_FREQS = DIFF_HEAD_DIM // 4

PEER_HEADS = 8
PEER_KEYS = 128
PEER_TOPK = 16
PEER_QDIM = 256
N_EXPERTS = PEER_KEYS * PEER_KEYS
PEER_TOKEN_BLOCK = 128
NORM_EPS = 1e-6

kernel_name = 'hybrid_gdn_hyena_diffattn_peer_dit'


def rmsnorm(x, g):
    xf = x.astype(jnp.float32)
    y = xf * lax.rsqrt(jnp.mean(xf * xf, axis=-1, keepdims=True) + NORM_EPS)
    return (y * g.astype(jnp.float32)).astype(x.dtype)


def modulate(h, shift, scale):
    return h * (1.0 + scale) + shift


def l2norm(x):
    x = x.astype(jnp.float32)
    return x * lax.rsqrt(jnp.sum(x * x, axis=-1, keepdims=True) + NORM_EPS)


def depthwise_conv(x, w):
    return lax.conv_general_dilated(x, w[:, None, :].astype(x.dtype), window_strides=(1,),
                                    padding=[(SHORT_CONV // 2, SHORT_CONV // 2)],
                                    dimension_numbers=('NWC', 'WIO', 'NWC'),
                                    feature_group_count=x.shape[-1])


def gated_delta_chunked(q, k, v, beta, g, s0):
    f32 = jnp.float32
    B, H, L, dk = q.shape
    dv = v.shape[-1]
    C = GDN_CHUNK
    n = L // C
    q, k, v, beta, g = [t.astype(f32).reshape(B, H, n, C, *t.shape[3:]) for t in (q, k, v, beta, g)]
    G = jnp.cumsum(g, axis=-1)
    causal = jnp.tril(jnp.ones((C, C), bool))
    strict = jnp.tril(jnp.ones((C, C), bool), -1)
    gdiff = G[..., :, None] - G[..., None, :]
    decay = jnp.where(causal, jnp.exp(jnp.where(causal, gdiff, 0.0)), 0.0)
    kk = jnp.einsum('bhncd,bhnjd->bhncj', k, k)
    a_mat = jnp.where(strict, beta[..., :, None] * kk * decay, 0.0)
    eye = jnp.eye(C, dtype=f32)
    rhs = jnp.concatenate([beta[..., None] * v, (beta * jnp.exp(G))[..., None] * k], axis=-1)
    sol = lax.linalg.triangular_solve(eye + a_mat, rhs, left_side=True, lower=True, unit_diagonal=True)
    w_intra, k_cum = sol[..., :dv], sol[..., dv:]
    qk = jnp.einsum('bhncd,bhnjd->bhncj', q, k) * decay
    q_dec = q * jnp.exp(G)[..., None]
    k_dec = k * jnp.exp(G[..., -1:] - G)[..., None]
    chunk_decay = jnp.exp(G[..., -1])

    def step(s, inp):
        w_i, kc_i, qk_i, qd_i, kd_i, cd_i = inp
        w = w_i - jnp.einsum('bhcd,bhde->bhce', kc_i, s)
        o = jnp.einsum('bhcd,bhde->bhce', qd_i, s) + jnp.einsum('bhcj,bhje->bhce', qk_i, w)
        s = s * cd_i[..., None, None] + jnp.einsum('bhcd,bhce->bhde', kd_i, w)
        return s, o

    xs = tuple(jnp.moveaxis(t, 2, 0) for t in (w_intra, k_cum, qk, q_dec, k_dec, chunk_decay))
    s_fin, o = lax.scan(step, s0.astype(f32), xs)
    o = jnp.moveaxis(o, 0, 2).reshape(B, H, L, dv)
    return o, s_fin


def gdn_inputs(qkv, ba, a_log, dt_bias):
    B, L, _ = qkv.shape
    qkv = jnp.transpose(jax.nn.silu(qkv).reshape(B, L, 3, GDN_HEADS, GDN_HEAD_DIM), (2, 0, 3, 1, 4))
    q = l2norm(qkv[0]) * (GDN_HEAD_DIM ** -0.5)
    k = l2norm(qkv[1])
    v = qkv[2]
    ba = ba.astype(jnp.float32).reshape(B, L, 2, 2, GDN_HEADS)
    beta = jax.nn.sigmoid(ba[:, :, 0])
    g = -jnp.exp(a_log.astype(jnp.float32)) * jax.nn.softplus(ba[:, :, 1] + dt_bias.astype(jnp.float32))
    return q, k, v, jnp.transpose(beta, (2, 0, 3, 1)), jnp.transpose(g, (2, 0, 3, 1))


def gdn_bidir(q, k, v, beta, g, s0):
    o_f, s_f = gated_delta_chunked(q, k, v, beta[0], g[0], s0[0])
    fl = lambda t: jnp.flip(t, axis=2)
    o_b, s_b = gated_delta_chunked(fl(q), fl(k), fl(v), fl(beta[1]), fl(g[1]), s0[1])
    return o_f + fl(o_b), jnp.stack([s_f, s_b])


def hyena_filters(L, w1, b1, w2, b2, w3, b3, w4, freq):
    f32 = jnp.float32
    t = jnp.linspace(0.0, 1.0, L, dtype=f32)
    bands = (HYENA_EMB - 1) // 2
    wpos = 2.0 * math.pi * jnp.arange(L, dtype=f32) / L
    fb = jnp.linspace(1e-4, bands - 1, bands, dtype=f32)
    z = jnp.concatenate([t[:, None], jnp.cos(wpos[:, None] * fb), -jnp.sin(wpos[:, None] * fb)], axis=-1)
    h = jnp.sin(freq * (z @ w1 + b1))
    h = jnp.sin(freq * (h @ w2 + b2))
    h = jnp.sin(freq * (h @ w3 + b3))
    h = (h @ w4).astype(f32).reshape(L, HYENA_ORDER, 2, HYENA_WIDTH)
    deltas = jnp.linspace(HYENA_MIN_DECAY, HYENA_MAX_DECAY, HYENA_WIDTH, dtype=f32)
    window = jnp.exp(-t[:, None] * jnp.abs(deltas))
    return h * window[:, None, None, :]


def two_sided_long_conv(u, h_fwd, h_bwd, bias):
    B, L, C = u.shape
    filt2 = jnp.concatenate([h_fwd, jnp.zeros((1, C), jnp.float32), jnp.flip(h_bwd[1:], axis=0)], axis=0)
    uf = jnp.fft.rfft(u.astype(jnp.float32), n=2 * L, axis=1)
    ff = jnp.fft.rfft(filt2, n=2 * L, axis=0)
    y = jnp.fft.irfft(uf * ff[None], n=2 * L, axis=1)[:, :L]
    return (y + u.astype(jnp.float32) * bias.astype(jnp.float32)).astype(u.dtype)


def hyena(xh, filt, bias):
    z = xh[:, :, 2]
    for o in range(HYENA_ORDER):
        z = xh[:, :, o] * two_sided_long_conv(z, filt[:, o, 0], filt[:, o, 1], bias[o])
    return z


def even_mixer(h_lat, h_ctx, need_ctx, w_in, conv_w, a_log, dt_bias, gdn_gain,
               hf_w1, hf_b1, hf_w2, hf_b2, hf_w3, hf_b3, hf_w4, hf_freq, hy_bias, w_out):
    def project(h):
        B, L, _ = h.shape
        p = h @ w_in
        cv = depthwise_conv(p[..., :N_CONV], conv_w)
        return (cv[..., :3 * GDN_WIDTH], cv[..., 3 * GDN_WIDTH:].reshape(B, L, 3, HYENA_WIDTH),
                p[..., N_CONV:N_CONV + GDN_WIDTH], p[..., N_CONV + GDN_WIDTH:])

    def gdn_out(o, z):
        B, H, L, dv = o.shape
        o = jnp.transpose(o, (0, 2, 1, 3)).astype(z.dtype)
        o = rmsnorm(o, gdn_gain) * jax.nn.silu(z.reshape(B, L, H, dv))
        return o.reshape(B, L, H * dv)

    def hyena_branch(xh):
        filt = hyena_filters(xh.shape[1], hf_w1, hf_b1, hf_w2, hf_b2, hf_w3, hf_b3, hf_w4, hf_freq)
        return hyena(xh, filt, hy_bias)

    B = h_ctx.shape[0]
    qkv_c, hy_c, z_c, ba_c = project(h_ctx)
    q, k, v, beta, g = gdn_inputs(qkv_c, ba_c, a_log, dt_bias)
    s0 = jnp.zeros((2, B, GDN_HEADS, GDN_HEAD_DIM, GDN_HEAD_DIM), jnp.float32)
    o_c, s_c = gdn_bidir(q, k, v, beta, g, s0)
    qkv_l, hy_l, z_l, ba_l = project(h_lat)
    q, k, v, beta, g = gdn_inputs(qkv_l, ba_l, a_log, dt_bias)
    o_l, _ = gdn_bidir(q, k, v, beta, g, s_c)
    y_lat = jnp.concatenate([gdn_out(o_l, z_l), hyena_branch(hy_l)], axis=-1) @ w_out
    y_ctx = None
    if need_ctx:
        y_ctx = jnp.concatenate([gdn_out(o_c, z_c), hyena_branch(hy_c)], axis=-1) @ w_out
    return y_lat, y_ctx


def axial_rope(L):
    rows = L // GRID_W
    r, cidx = jnp.meshgrid(jnp.arange(rows), jnp.arange(GRID_W), indexing='ij')
    pos = jnp.stack([r.reshape(-1), cidx.reshape(-1)], axis=-1).astype(jnp.float32)
    inv = ROPE_BASE ** (-jnp.arange(ROPE_FREQS, dtype=jnp.float32) / ROPE_FREQS)
    ang = pos[:, :, None] * inv
    return jnp.cos(ang), jnp.sin(ang)


def apply_rope(x, cos, sin):
    xs = x.reshape(*x.shape[:-1], 2, 2, ROPE_FREQS)
    x1, x2 = xs[..., 0, :], xs[..., 1, :]
    cs = cos[None, :, None, None].astype(x.dtype)
    sn = sin[None, :, None, None].astype(x.dtype)
    return jnp.stack([x1 * cs - x2 * sn, x2 * cs + x1 * sn], axis=-2).reshape(x.shape)


def odd_mixer(h_lat, h_ctx, need_ctx, layer_idx, w_qkv, lam_q1, lam_k1, lam_q2, lam_k2, subln, w_out):
    f32 = jnp.float32
    lam_init = 0.8 - 0.6 * math.exp(-0.3 * layer_idx)
    lam = (jnp.exp(jnp.sum(lam_q1.astype(f32) * lam_k1.astype(f32)))
           - jnp.exp(jnp.sum(lam_q2.astype(f32) * lam_k2.astype(f32))) + lam_init)
    scale = DIFF_HEAD_DIM ** -0.5

    def project(h):
        B, L, _ = h.shape
        p = h @ w_qkv
        q = p[..., :DIFF_QK_WIDTH].reshape(B, L, DIFF_HEADS, 2, DIFF_HEAD_DIM)
        k = p[..., DIFF_QK_WIDTH:2 * DIFF_QK_WIDTH].reshape(B, L, DIFF_HEADS, 2, DIFF_HEAD_DIM)
        v = p[..., 2 * DIFF_QK_WIDTH:].reshape(B, L, DIFF_HEADS, DIFF_V_DIM)
        return q, k, v

    to_bhm = lambda t: jnp.transpose(t, (0, 2, 3, 1, 4))
    to_bh = lambda t: jnp.transpose(t, (0, 2, 1, 3))

    def attend(qb, kk, vv):
        s = jnp.einsum('bhmqd,bhmkd->bhmqk', qb, kk).astype(f32) * scale
        p = jax.nn.softmax(s, axis=-1)
        a = p[:, :, 0] - lam * p[:, :, 1]
        return jnp.einsum('bhqk,bhkd->bhqd', a.astype(vv.dtype), vv)

    def head_out(o):
        B, H, L, dv = o.shape
        o = rmsnorm(jnp.transpose(o, (0, 2, 1, 3)), subln) * (1.0 - lam_init)
        return o.reshape(B, L, H * dv) @ w_out

    q_c, k_c, v_c = project(h_ctx)
    q_l, k_l, v_l = project(h_lat)
    B, L, _ = h_lat.shape
    cos, sin = axial_rope(L)
    q_l = apply_rope(q_l, cos, sin)
    k_l = apply_rope(k_l, cos, sin)
    k_all = jnp.concatenate([to_bhm(k_c), to_bhm(k_l)], axis=3)
    v_all = jnp.concatenate([to_bh(v_c), to_bh(v_l)], axis=2)
    nb = L // Q_BLOCK
    qb = jnp.moveaxis(to_bhm(q_l).reshape(B, DIFF_HEADS, 2, nb, Q_BLOCK, DIFF_HEAD_DIM), 3, 0)
    o_l = lax.map(lambda blk: attend(blk, k_all, v_all), qb)
    o_l = jnp.moveaxis(o_l, 0, 2).reshape(B, DIFF_HEADS, L, DIFF_V_DIM)
    y_lat = head_out(o_l)
    y_ctx = None
    if need_ctx:
        y_ctx = head_out(attend(to_bhm(q_c), to_bhm(k_c), to_bh(v_c)))
    return y_lat, y_ctx


def peer_ffn(h, w_q, keys, u_tab, v_tab):
    T, D = h.shape
    blocks = h.reshape(T // PEER_TOKEN_BLOCK, PEER_TOKEN_BLOCK, D)

    def one(hb):
        t = hb.shape[0]
        q = (hb @ w_q).reshape(t, PEER_HEADS, 2, PEER_QDIM // 2)
        s = jnp.einsum('thps,hpns->thpn', q, keys).astype(jnp.float32)
        sv, si = lax.top_k(s, PEER_TOPK)
        cand = (sv[:, :, 0, :, None] + sv[:, :, 1, None, :]).reshape(t, PEER_HEADS, PEER_TOPK * PEER_TOPK)
        cidx = (si[:, :, 0, :, None] * PEER_KEYS + si[:, :, 1, None, :]).reshape(t, PEER_HEADS, PEER_TOPK * PEER_TOPK)
        top_s, pos = lax.top_k(cand, PEER_TOPK)
        idx = jnp.take_along_axis(cidx, pos, axis=-1)
        gate = jax.nn.softmax(top_s, axis=-1)
        u = u_tab[idx]
        act = jax.nn.gelu(jnp.einsum('td,thkd->thk', hb, u).astype(jnp.float32), approximate=False)
        coef = (gate * act).astype(hb.dtype)
        return jnp.einsum('thk,thkd->td', coef, v_tab[idx])

    return lax.map(one, blocks).reshape(T, D)


def setup_inputs(seed: int = 0) -> dict:
    key = jax.random.key(seed)
    keys = iter(jax.random.split(key, 48))
    f32 = jnp.float32
    D = D_MODEL

    def nrm(shape, scale):
        return jax.random.normal(next(keys), shape, f32) * scale

    def gain(shape):
        return 1.0 + nrm(shape, 0.02)

    x = nrm((BATCH, SEQ, D), 1.0)
    c = nrm((BATCH, D), 1.0)
    ctx = nrm((BATCH, CTX_LEN, D), 1.0)
    c_ctx = nrm((D,), 1.0)
    norm1 = gain((DEPTH, D))
    norm2 = gain((DEPTH, D))
    w_ada = nrm((DEPTH, D, 6 * D), 0.5 * D ** -0.5)
    b_ada = nrm((DEPTH, 6 * D), 0.01)
    w_in = nrm((N_EVEN, D, N_IN), D ** -0.5)
    conv_w = nrm((N_EVEN, SHORT_CONV, N_CONV), SHORT_CONV ** -0.5)
    a_log = jnp.log(jax.random.uniform(next(keys), (N_EVEN, 2, GDN_HEADS), f32, minval=1.0, maxval=16.0))
    dt = jnp.exp(jax.random.uniform(next(keys), (N_EVEN, 2, GDN_HEADS), f32,
                                    minval=math.log(1e-3), maxval=math.log(1e-1)))
    dt_bias = dt + jnp.log(-jnp.expm1(-dt))
    gdn_gain = gain((N_EVEN, GDN_HEAD_DIM))
    hf_w1 = nrm((N_EVEN, HYENA_EMB, HYENA_HIDDEN), HYENA_EMB ** -0.5)
    hf_b1 = nrm((N_EVEN, HYENA_HIDDEN), 0.1)
    hf_w2 = nrm((N_EVEN, HYENA_HIDDEN, HYENA_HIDDEN), HYENA_HIDDEN ** -0.5)
    hf_b2 = nrm((N_EVEN, HYENA_HIDDEN), 0.1)
    hf_w3 = nrm((N_EVEN, HYENA_HIDDEN, HYENA_HIDDEN), HYENA_HIDDEN ** -0.5)
    hf_b3 = nrm((N_EVEN, HYENA_HIDDEN), 0.1)
    hf_w4 = nrm((N_EVEN, HYENA_HIDDEN, HYENA_ORDER * 2 * HYENA_WIDTH), 0.02)
    hf_freq = 1.0 + nrm((N_EVEN, HYENA_HIDDEN), 0.1)
    hy_bias = nrm((N_EVEN, HYENA_ORDER, HYENA_WIDTH), 0.5)
    w_out_even = nrm((N_EVEN, GDN_WIDTH + HYENA_WIDTH, D), (GDN_WIDTH + HYENA_WIDTH) ** -0.5)
    w_qkv = nrm((N_ODD, D, 2 * DIFF_QK_WIDTH + DIFF_V_WIDTH), D ** -0.5)
    lam_q1 = nrm((N_ODD, DIFF_HEAD_DIM), 0.1)
    lam_k1 = nrm((N_ODD, DIFF_HEAD_DIM), 0.1)
    lam_q2 = nrm((N_ODD, DIFF_HEAD_DIM), 0.1)
    lam_k2 = nrm((N_ODD, DIFF_HEAD_DIM), 0.1)
    subln = gain((N_ODD, DIFF_V_DIM))
    w_out_odd = nrm((N_ODD, DIFF_V_WIDTH, D), DIFF_V_WIDTH ** -0.5)
    peer_wq = nrm((DEPTH, D, PEER_HEADS * PEER_QDIM), D ** -0.5)
    peer_keys = nrm((DEPTH, PEER_HEADS, 2, PEER_KEYS, PEER_QDIM // 2), (PEER_QDIM // 2) ** -0.5)
    peer_u = nrm((DEPTH, N_EXPERTS, D), D ** -0.5)
    peer_v = nrm((DEPTH, N_EXPERTS, D), PEER_TOPK ** -0.5)
    final_norm = gain((D,))
    return {'x': x, 'c': c, 'ctx': ctx, 'c_ctx': c_ctx, 'norm1': norm1, 'norm2': norm2,
            'w_ada': w_ada, 'b_ada': b_ada, 'w_in': w_in, 'conv_w': conv_w, 'a_log': a_log,
            'dt_bias': dt_bias, 'gdn_gain': gdn_gain, 'hf_w1': hf_w1, 'hf_b1': hf_b1, 'hf_w2': hf_w2,
            'hf_b2': hf_b2, 'hf_w3': hf_w3, 'hf_b3': hf_b3, 'hf_w4': hf_w4, 'hf_freq': hf_freq,
            'hy_bias': hy_bias, 'w_out_even': w_out_even, 'w_qkv': w_qkv, 'lam_q1': lam_q1,
            'lam_k1': lam_k1, 'lam_q2': lam_q2, 'lam_k2': lam_k2, 'subln': subln,
            'w_out_odd': w_out_odd, 'peer_wq': peer_wq, 'peer_keys': peer_keys, 'peer_u': peer_u,
            'peer_v': peer_v, 'final_norm': final_norm}


def reference(x, c, ctx, c_ctx, norm1, norm2, w_ada, b_ada, w_in, conv_w, a_log, dt_bias, gdn_gain,
              hf_w1, hf_b1, hf_w2, hf_b2, hf_w3, hf_b3, hf_w4, hf_freq, hy_bias, w_out_even,
              w_qkv, lam_q1, lam_k1, lam_q2, lam_k2, subln, w_out_odd,
              peer_wq, peer_keys, peer_u, peer_v, final_norm):
    B, L, D = x.shape
    x_lat, x_ctx = x, ctx
    for layer in range(DEPTH):
        last = layer == DEPTH - 1
        m_l = (jax.nn.silu(c) @ w_ada[layer] + b_ada[layer]).reshape(B, 6, 1, D)
        m_c = (jax.nn.silu(c_ctx) @ w_ada[layer] + b_ada[layer]).reshape(6, D)
        h_l = modulate(rmsnorm(x_lat, norm1[layer]), m_l[:, 0], m_l[:, 1])
        h_c = modulate(rmsnorm(x_ctx, norm1[layer]), m_c[0], m_c[1])
        if layer % 2 == 0:
            e = layer // 2
            y_l, y_c = even_mixer(h_l, h_c, not last, w_in[e], conv_w[e], a_log[e], dt_bias[e], gdn_gain[e],
                                  hf_w1[e], hf_b1[e], hf_w2[e], hf_b2[e], hf_w3[e], hf_b3[e], hf_w4[e],
                                  hf_freq[e], hy_bias[e], w_out_even[e])
        else:
            o = layer // 2
            y_l, y_c = odd_mixer(h_l, h_c, not last, layer, w_qkv[o], lam_q1[o], lam_k1[o], lam_q2[o],
                                 lam_k2[o], subln[o], w_out_odd[o])
        x_lat = x_lat + m_l[:, 2] * y_l
        h_l = modulate(rmsnorm(x_lat, norm2[layer]), m_l[:, 3], m_l[:, 4]).reshape(B * L, D)
        if last:
            f_l = peer_ffn(h_l, peer_wq[layer], peer_keys[layer], peer_u[layer], peer_v[layer])
        else:
            x_ctx = x_ctx + m_c[2] * y_c
            h_c = modulate(rmsnorm(x_ctx, norm2[layer]), m_c[3], m_c[4]).reshape(-1, D)
            f = peer_ffn(jnp.concatenate([h_l, h_c], axis=0), peer_wq[layer], peer_keys[layer],
                         peer_u[layer], peer_v[layer])
            f_l = f[:B * L]
            x_ctx = x_ctx + m_c[5] * f[B * L:].reshape(x_ctx.shape)
        x_lat = x_lat + m_l[:, 5] * f_l.reshape(B, L, D)
    return rmsnorm(x_lat, final_norm)
```

```python
import functools
import math

import jax
import jax.numpy as jnp
from jax import lax
from jax.experimental import pallas as pl
from jax.experimental.pallas import tpu as pltpu

F32 = jnp.float32
BF16 = jnp.bfloat16

D_MODEL = 1024
BATCH = 8
SEQ = 4096
DEPTH = 4
GRID_W = 64
CTX_LEN = 256
SEQ_ALL = CTX_LEN + SEQ
ROW_BLOCK = CTX_LEN
N_ROW_BLOCKS = SEQ_ALL // ROW_BLOCK

GDN_HEAD_DIM = 128
GDN_WIDTH = D_MODEL // 2
GDN_HEADS = GDN_WIDTH // GDN_HEAD_DIM
GDN_CHUNK = 64
SHORT_CONV = 3

HYENA_WIDTH = D_MODEL // 2
HYENA_ORDER = 2
HYENA_EMB = 33
HYENA_DECAY_TARGET = 1e-2
HYENA_MIN_DECAY = math.log(HYENA_DECAY_TARGET) / 1.5
HYENA_MAX_DECAY = math.log(HYENA_DECAY_TARGET) / 0.3

N_CONV = 3 * GDN_WIDTH + 3 * HYENA_WIDTH
N_IN = N_CONV + GDN_WIDTH + 4 * GDN_HEADS
LANES = 128
N_IN_PAD = -(-N_IN // LANES) * LANES

DIFF_HEAD_DIM = 64
DIFF_HEADS = D_MODEL // (2 * DIFF_HEAD_DIM)
DIFF_V_DIM = 2 * DIFF_HEAD_DIM
DIFF_QK_WIDTH = DIFF_HEADS * 2 * DIFF_HEAD_DIM
DIFF_V_WIDTH = DIFF_HEADS * DIFF_V_DIM
ROPE_BASE = 10000.0
ROPE_FREQS = DIFF_HEAD_DIM // 4

PEER_HEADS = 8
PEER_KEYS = 128
PEER_TOPK = 16
PEER_QDIM = 256
PEER_TOKEN_BLOCK = 128
NORM_EPS = 1e-6

VMEM_LIMIT = 56 * 1024 * 1024


def _cparams(*sem):
    return pltpu.CompilerParams(dimension_semantics=sem, vmem_limit_bytes=VMEM_LIMIT)


def _norm_mod(x, g, mod):
    y = x * lax.rsqrt(jnp.mean(x * x, axis=-1, keepdims=True) + NORM_EPS) * g
    return y * (1.0 + mod[1:2, :]) + mod[0:1, :]


def _proj_kernel(x_ref, g_ref, mod_ref, w_ref, o_ref):
    h = _norm_mod(x_ref[0], g_ref[...], mod_ref[0, 0])
    o_ref[0] = jnp.dot(h.astype(BF16), w_ref[...], preferred_element_type=F32).astype(o_ref.dtype)


def _proj_h_kernel(x_ref, g_ref, mod_ref, w_ref, h_ref, o_ref):
    h = _norm_mod(x_ref[0], g_ref[...], mod_ref[0, 0])
    h_ref[0] = h
    o_ref[0] = jnp.dot(h.astype(BF16), w_ref[...], preferred_element_type=F32)


def _proj_rope_kernel(x_ref, g_ref, mod_ref, w_ref, cos_ref, sin_ref, o_ref):
    h = _norm_mod(x_ref[0], g_ref[...], mod_ref[0, 0])
    p = jnp.dot(h.astype(BF16), w_ref[...], preferred_element_type=F32)
    cos = cos_ref[...]
    sin = sin_ref[...]
    lane = lax.broadcasted_iota(jnp.int32, (1, LANES), 1)
    first_half = (lane % (2 * ROPE_FREQS)) < ROPE_FREQS
    n_rot = 2 * DIFF_QK_WIDTH // LANES
    for j in range(p.shape[1] // LANES):
        blk = p[:, j * LANES:(j + 1) * LANES]
        if j < n_rot:
            partner = jnp.where(first_half, pltpu.roll(blk, LANES - ROPE_FREQS, 1),
                                pltpu.roll(blk, ROPE_FREQS, 1))
            blk = blk * cos + partner * sin
            if j < n_rot // 2:
                blk = blk * (DIFF_HEAD_DIM ** -0.5)
        o_ref[0, :, j * LANES:(j + 1) * LANES] = blk.astype(o_ref.dtype)


def _row_specs(n_out):
    x_spec = pl.BlockSpec((1, ROW_BLOCK, D_MODEL), lambda b, s: (b, s, 0))
    g_spec = pl.BlockSpec((1, D_MODEL), lambda b, s: (0, 0))
    mod_spec = pl.BlockSpec((1, 1, 2, D_MODEL), lambda b, s: (b, jnp.minimum(s, 1), 0, 0))
    w_spec = pl.BlockSpec((D_MODEL, n_out), lambda b, s: (0, 0))
    o_spec = pl.BlockSpec((1, ROW_BLOCK, n_out), lambda b, s: (b, s, 0))
    return x_spec, g_spec, mod_spec, w_spec, o_spec


def norm_mod_proj(x, g, mod, w, out_dtype=F32):
    n_out = w.shape[1]
    x_spec, g_spec, mod_spec, w_spec, o_spec = _row_specs(n_out)
    return pl.pallas_call(
        _proj_kernel, grid=(BATCH, N_ROW_BLOCKS),
        in_specs=[x_spec, g_spec, mod_spec, w_spec], out_specs=o_spec,
        out_shape=jax.ShapeDtypeStruct((BATCH, SEQ_ALL, n_out), out_dtype),
        compiler_params=_cparams("parallel", "parallel"), name="norm_mod_proj",
    )(x, g.reshape(1, D_MODEL), mod, w)


def norm_mod_proj_h(x, g, mod, w):
    n_out = w.shape[1]
    x_spec, g_spec, mod_spec, w_spec, o_spec = _row_specs(n_out)
    return pl.pallas_call(
        _proj_h_kernel, grid=(BATCH, N_ROW_BLOCKS),
        in_specs=[x_spec, g_spec, mod_spec, w_spec], out_specs=[x_spec, o_spec],
        out_shape=[jax.ShapeDtypeStruct((BATCH, SEQ_ALL, D_MODEL), F32),
                   jax.ShapeDtypeStruct((BATCH, SEQ_ALL, n_out), F32)],
        compiler_params=_cparams("parallel", "parallel"), name="norm_mod_proj_h",
    )(x, g.reshape(1, D_MODEL), mod, w)


def norm_mod_proj_rope(x, g, mod, w, cos_tab, sin_tab):
    n_out = w.shape[1]
    x_spec, g_spec, mod_spec, w_spec, o_spec = _row_specs(n_out)
    tab_spec = pl.BlockSpec((ROW_BLOCK, LANES), lambda b, s: (s, 0))
    return pl.pallas_call(
        _proj_rope_kernel, grid=(BATCH, N_ROW_BLOCKS),
        in_specs=[x_spec, g_spec, mod_spec, w_spec, tab_spec, tab_spec], out_specs=o_spec,
        out_shape=jax.ShapeDtypeStruct((BATCH, SEQ_ALL, n_out), BF16),
        compiler_params=_cparams("parallel", "parallel"), name="norm_mod_proj_rope",
    )(x, g.reshape(1, D_MODEL), mod, w, cos_tab, sin_tab)


def _out_proj_kernel(a_ref, w_ref, x_ref, gate_ref, o_ref):
    y = jnp.dot(a_ref[0].astype(BF16), w_ref[...], preferred_element_type=F32)
    o_ref[0] = x_ref[0] + gate_ref[0, 0] * y


def out_proj_residual(a, w, x, gate):
    k = a.shape[-1]
    return pl.pallas_call(
        _out_proj_kernel, grid=(BATCH, N_ROW_BLOCKS),
        in_specs=[pl.BlockSpec((1, ROW_BLOCK, k), lambda b, s: (b, s, 0)),
                  pl.BlockSpec((k, D_MODEL), lambda b, s: (0, 0)),
                  pl.BlockSpec((1, ROW_BLOCK, D_MODEL), lambda b, s: (b, s, 0)),
                  pl.BlockSpec((1, 1, 1, D_MODEL), lambda b, s: (b, jnp.minimum(s, 1), 0, 0))],
        out_specs=pl.BlockSpec((1, ROW_BLOCK, D_MODEL), lambda b, s: (b, s, 0)),
        out_shape=jax.ShapeDtypeStruct((BATCH, SEQ_ALL, D_MODEL), F32),
        compiler_params=_cparams("parallel", "parallel"), name="out_proj_residual",
    )(a, w, x, gate)


def _diff_attn_kernel(lam_ref, q_ref, k_ref, v_ref, gain_ref, o_ref, *, out_scale):
    lam = lam_ref[0]
    lane = lax.broadcasted_iota(jnp.int32, (1, LANES), 1)
    in_map0 = lane < DIFF_HEAD_DIM

    def attend(n_keys):
        q = q_ref[0]
        k = k_ref[0, :n_keys, :]
        v = v_ref[0, :n_keys, :]
        zero = jnp.zeros_like(q)
        nt = (((1,), (1,)), ((), ()))
        s0 = lax.dot_general(jnp.where(in_map0, q, zero), k, nt, preferred_element_type=F32)
        s1 = lax.dot_general(jnp.where(in_map0, zero, q), k, nt, preferred_element_type=F32)
        p0 = jnp.exp(s0 - jnp.max(s0, axis=-1, keepdims=True))
        p1 = jnp.exp(s1 - jnp.max(s1, axis=-1, keepdims=True))
        r0 = 1.0 / jnp.sum(p0, axis=-1, keepdims=True)
        r1 = lam / jnp.sum(p1, axis=-1, keepdims=True)
        a = p0 * r0 - p1 * r1
        o = jnp.dot(a.astype(BF16), v, preferred_element_type=F32)
        o = o * lax.rsqrt(jnp.mean(o * o, axis=-1, keepdims=True) + NORM_EPS)
        o_ref[0] = (o * gain_ref[...] * out_scale).astype(o_ref.dtype)

    is_ctx = pl.program_id(2) == 0

    @pl.when(is_ctx)
    def _():
        attend(CTX_LEN)

    @pl.when(jnp.logical_not(is_ctx))
    def _():
        attend(SEQ_ALL)


def diff_attention(qkv, lam, gain, out_scale):
    kv_blocks = DIFF_QK_WIDTH // LANES
    return pl.pallas_call(
        functools.partial(_diff_attn_kernel, out_scale=out_scale),
        grid=(BATCH, DIFF_HEADS, N_ROW_BLOCKS),
        in_specs=[pl.BlockSpec(memory_space=pltpu.SMEM),
                  pl.BlockSpec((1, ROW_BLOCK, LANES), lambda b, h, s: (b, s, h)),
                  pl.BlockSpec((1, SEQ_ALL, LANES), lambda b, h, s: (b, 0, kv_blocks + h)),
                  pl.BlockSpec((1, SEQ_ALL, LANES), lambda b, h, s: (b, 0, 2 * kv_blocks + h)),
                  pl.BlockSpec((1, LANES), lambda b, h, s: (0, 0))],
        out_specs=pl.BlockSpec((1, ROW_BLOCK, LANES), lambda b, h, s: (b, s, h)),
        out_shape=jax.ShapeDtypeStruct((BATCH, SEQ_ALL, DIFF_V_WIDTH), BF16),
        compiler_params=_cparams("parallel", "parallel", "arbitrary"), name="diff_attention",
    )(lam.reshape(1), qkv, qkv, qkv, gain.reshape(1, LANES))


def rope_tables():
    t = jnp.arange(SEQ)
    pos = jnp.stack([t // GRID_W, t % GRID_W], axis=-1).astype(F32)
    inv = ROPE_BASE ** (-jnp.arange(ROPE_FREQS, dtype=F32) / ROPE_FREQS)
    ang = pos[:, :, None] * inv
    cos = jnp.cos(ang)
    sin = jnp.sin(ang)
    cos_l = jnp.tile(jnp.concatenate([cos, cos], axis=-1).reshape(SEQ, 4 * ROPE_FREQS), (1, 2))
    sin_l = jnp.tile(jnp.concatenate([-sin, sin], axis=-1).reshape(SEQ, 4 * ROPE_FREQS), (1, 2))
    cos_t = jnp.concatenate([jnp.ones((CTX_LEN, LANES), F32), cos_l], axis=0)
    sin_t = jnp.concatenate([jnp.zeros((CTX_LEN, LANES), F32), sin_l], axis=0)
    return cos_t, sin_t


def _rmsnorm(x, g):
    xf = x.astype(F32)
    y = xf * lax.rsqrt(jnp.mean(xf * xf, axis=-1, keepdims=True) + NORM_EPS)
    return (y * g.astype(F32)).astype(x.dtype)


def _l2norm(x):
    x = x.astype(F32)
    return x * lax.rsqrt(jnp.sum(x * x, axis=-1, keepdims=True) + NORM_EPS)


def _depthwise_conv(x, w):
    return lax.conv_general_dilated(x, w[:, None, :].astype(x.dtype), window_strides=(1,),
                                    padding=[(SHORT_CONV // 2, SHORT_CONV // 2)],
                                    dimension_numbers=('NWC', 'WIO', 'NWC'),
                                    feature_group_count=x.shape[-1])


def _gated_delta_chunked(q, k, v, beta, g, s0):
    B, H, L, dk = q.shape
    dv = v.shape[-1]
    C = GDN_CHUNK
    n = L // C
    hp = lax.Precision.HIGHEST
    q, k, v, beta, g = [t.astype(F32).reshape(B, H, n, C, *t.shape[3:]) for t in (q, k, v, beta, g)]
    G = jnp.cumsum(g, axis=-1)
    causal = jnp.tril(jnp.ones((C, C), bool))
    strict = jnp.tril(jnp.ones((C, C), bool), -1)
    gdiff = G[..., :, None] - G[..., None, :]
    decay = jnp.where(causal, jnp.exp(jnp.where(causal, gdiff, 0.0)), 0.0)
    kk = jnp.einsum('bhncd,bhnjd->bhncj', k, k, precision=hp)
    a_mat = jnp.where(strict, beta[..., :, None] * kk * decay, 0.0)
    eye = jnp.eye(C, dtype=F32)
    rhs = jnp.concatenate([beta[..., None] * v, (beta * jnp.exp(G))[..., None] * k], axis=-1)
    sol = lax.linalg.triangular_solve(eye + a_mat, rhs, left_side=True, lower=True, unit_diagonal=True)
    w_intra, k_cum = sol[..., :dv], sol[..., dv:]
    qk = jnp.einsum('bhncd,bhnjd->bhncj', q, k, precision=hp) * decay
    q_dec = q * jnp.exp(G)[..., None]
    k_dec = k * jnp.exp(G[..., -1:] - G)[..., None]
    chunk_decay = jnp.exp(G[..., -1])

    def step(s, inp):
        w_i, kc_i, qk_i, qd_i, kd_i, cd_i = inp
        w = w_i - jnp.einsum('bhcd,bhde->bhce', kc_i, s, precision=hp)
        o = (jnp.einsum('bhcd,bhde->bhce', qd_i, s, precision=hp)
             + jnp.einsum('bhcj,bhje->bhce', qk_i, w, precision=hp))
        s = s * cd_i[..., None, None] + jnp.einsum('bhcd,bhce->bhde', kd_i, w, precision=hp)
        return s, o

    xs = tuple(jnp.moveaxis(t, 2, 0) for t in (w_intra, k_cum, qk, q_dec, k_dec, chunk_decay))
    s_fin, o = lax.scan(step, s0.astype(F32), xs)
    o = jnp.moveaxis(o, 0, 2).reshape(B, H, L, dv)
    return o, s_fin


def _gdn_inputs(qkv, ba, a_log, dt_bias):
    B, L, _ = qkv.shape
    qkv = jnp.transpose(jax.nn.silu(qkv).reshape(B, L, 3, GDN_HEADS, GDN_HEAD_DIM), (2, 0, 3, 1, 4))
    q = _l2norm(qkv[0]) * (GDN_HEAD_DIM ** -0.5)
    k = _l2norm(qkv[1])
    v = qkv[2]
    ba = ba.astype(F32).reshape(B, L, 2, 2, GDN_HEADS)
    beta = jax.nn.sigmoid(ba[:, :, 0])
    g = -jnp.exp(a_log.astype(F32)) * jax.nn.softplus(ba[:, :, 1] + dt_bias.astype(F32))
    return q, k, v, jnp.transpose(beta, (2, 0, 3, 1)), jnp.transpose(g, (2, 0, 3, 1))


def _gdn_bidir(q, k, v, beta, g, s0):
    o_f, s_f = _gated_delta_chunked(q, k, v, beta[0], g[0], s0[0])
    fl = lambda t: jnp.flip(t, axis=2)
    o_b, s_b = _gated_delta_chunked(fl(q), fl(k), fl(v), fl(beta[1]), fl(g[1]), s0[1])
    return o_f + fl(o_b), jnp.stack([s_f, s_b])


def _hyena_filters(L, w1, b1, w2, b2, w3, b3, w4, freq):
    hp = lax.Precision.HIGHEST
    t = jnp.linspace(0.0, 1.0, L, dtype=F32)
    bands = (HYENA_EMB - 1) // 2
    wpos = 2.0 * math.pi * jnp.arange(L, dtype=F32) / L
    fb = jnp.linspace(1e-4, bands - 1, bands, dtype=F32)
    z = jnp.concatenate([t[:, None], jnp.cos(wpos[:, None] * fb), -jnp.sin(wpos[:, None] * fb)], axis=-1)
    h = jnp.sin(freq * (jnp.dot(z, w1, precision=hp) + b1))
    h = jnp.sin(freq * (jnp.dot(h, w2, precision=hp) + b2))
    h = jnp.sin(freq * (jnp.dot(h, w3, precision=hp) + b3))
    h = jnp.dot(h, w4, precision=hp).astype(F32).reshape(L, HYENA_ORDER, 2, HYENA_WIDTH)
    deltas = jnp.linspace(HYENA_MIN_DECAY, HYENA_MAX_DECAY, HYENA_WIDTH, dtype=F32)
    window = jnp.exp(-t[:, None] * jnp.abs(deltas))
    return h * window[:, None, None, :]


def _two_sided_long_conv(u, h_fwd, h_bwd, bias):
    B, L, C = u.shape
    filt2 = jnp.concatenate([h_fwd, jnp.zeros((1, C), F32), jnp.flip(h_bwd[1:], axis=0)], axis=0)
    uf = jnp.fft.rfft(u.astype(F32), n=2 * L, axis=1)
    ff = jnp.fft.rfft(filt2, n=2 * L, axis=0)
    y = jnp.fft.irfft(uf * ff[None], n=2 * L, axis=1)[:, :L]
    return (y + u.astype(F32) * bias.astype(F32)).astype(u.dtype)


def _hyena(xh, filt, bias):
    z = xh[:, :, 2]
    for o in range(HYENA_ORDER):
        z = xh[:, :, o] * _two_sided_long_conv(z, filt[:, o, 0], filt[:, o, 1], bias[o])
    return z


def _even_mixer_core(p, conv_w, a_log, dt_bias, gdn_gain, hf, hy_bias):
    def split(pp):
        B, L, _ = pp.shape
        cv = _depthwise_conv(pp[..., :N_CONV], conv_w)
        return (cv[..., :3 * GDN_WIDTH], cv[..., 3 * GDN_WIDTH:].reshape(B, L, 3, HYENA_WIDTH),
                pp[..., N_CONV:N_CONV + GDN_WIDTH], pp[..., N_CONV + GDN_WIDTH:])

    def gdn_out(o, z):
        B, H, L, dv = o.shape
        o = jnp.transpose(o, (0, 2, 1, 3)).astype(z.dtype)
        o = _rmsnorm(o, gdn_gain) * jax.nn.silu(z.reshape(B, L, H, dv))
        return o.reshape(B, L, H * dv)

    def hyena_branch(xh):
        return _hyena(xh, _hyena_filters(xh.shape[1], *hf), hy_bias)

    B = p.shape[0]
    qkv_c, hy_c, z_c, ba_c = split(p[:, :CTX_LEN])
    q, k, v, beta, g = _gdn_inputs(qkv_c, ba_c, a_log, dt_bias)
    s0 = jnp.zeros((2, B, GDN_HEADS, GDN_HEAD_DIM, GDN_HEAD_DIM), F32)
    o_c, s_c = _gdn_bidir(q, k, v, beta, g, s0)
    qkv_l, hy_l, z_l, ba_l = split(p[:, CTX_LEN:])
    q, k, v, beta, g = _gdn_inputs(qkv_l, ba_l, a_log, dt_bias)
    o_l, _ = _gdn_bidir(q, k, v, beta, g, s_c)
    a_lat = jnp.concatenate([gdn_out(o_l, z_l), hyena_branch(hy_l)], axis=-1)
    a_ctx = jnp.concatenate([gdn_out(o_c, z_c), hyena_branch(hy_c)], axis=-1)
    return jnp.concatenate([a_ctx, a_lat], axis=1)


def _peer_from_query(h, q, keys, u_tab, v_tab):
    T, D = h.shape
    nb = T // PEER_TOKEN_BLOCK
    hb_all = h.reshape(nb, PEER_TOKEN_BLOCK, D)
    qb_all = q.reshape(nb, PEER_TOKEN_BLOCK, PEER_HEADS, 2, PEER_QDIM // 2)

    def one(args):
        hb, qb = args
        t = hb.shape[0]
        s = jnp.einsum('thps,hpns->thpn', qb, keys, precision=lax.Precision.HIGHEST).astype(F32)
        sv, si = lax.top_k(s, PEER_TOPK)
        cand = (sv[:, :, 0, :, None] + sv[:, :, 1, None, :]).reshape(t, PEER_HEADS, PEER_TOPK * PEER_TOPK)
        cidx = (si[:, :, 0, :, None] * PEER_KEYS + si[:, :, 1, None, :]).reshape(t, PEER_HEADS, PEER_TOPK * PEER_TOPK)
        top_s, pos = lax.top_k(cand, PEER_TOPK)
        idx = jnp.take_along_axis(cidx, pos, axis=-1)
        gate = jax.nn.softmax(top_s, axis=-1)
        u = u_tab[idx]
        act = jax.nn.gelu(jnp.einsum('td,thkd->thk', hb, u).astype(F32), approximate=False)
        coef = (gate * act).astype(hb.dtype)
        return jnp.einsum('thk,thkd->td', coef, v_tab[idx])

    return lax.map(one, (hb_all, qb_all)).reshape(T, D)


def kernel(x, c, ctx, c_ctx, norm1, norm2, w_ada, b_ada, w_in, conv_w, a_log, dt_bias, gdn_gain, hf_w1, hf_b1, hf_w2, hf_b2, hf_w3, hf_b3, hf_w4, hf_freq, hy_bias, w_out_even, w_qkv, lam_q1, lam_k1, lam_q2, lam_k2, subln, w_out_odd, peer_wq, peer_keys, peer_u, peer_v, final_norm):
    hp = lax.Precision.HIGHEST
    xs = jnp.concatenate([ctx, x], axis=1)
    cos_tab, sin_tab = rope_tables()
    cond = jnp.concatenate([jax.nn.silu(c_ctx)[None], jax.nn.silu(c)], axis=0)
    for layer in range(DEPTH):
        m = (jnp.dot(cond, w_ada[layer], precision=hp) + b_ada[layer]).reshape(BATCH + 1, 6, D_MODEL)
        m = jnp.stack([jnp.broadcast_to(m[0], (BATCH, 6, D_MODEL)), m[1:]], axis=1)
        mod1, gate1 = m[:, :, 0:2], m[:, :, 2:3]
        mod2, gate2 = m[:, :, 3:5], m[:, :, 5:6]
        if layer % 2 == 0:
            e = layer // 2
            w_pad = jnp.pad(w_in[e], ((0, 0), (0, N_IN_PAD - N_IN))).astype(BF16)
            p = norm_mod_proj(xs, norm1[layer], mod1, w_pad)[..., :N_IN]
            hf = (hf_w1[e], hf_b1[e], hf_w2[e], hf_b2[e], hf_w3[e], hf_b3[e], hf_w4[e], hf_freq[e])
            a = _even_mixer_core(p, conv_w[e], a_log[e], dt_bias[e], gdn_gain[e], hf, hy_bias[e])
            xs = out_proj_residual(a, w_out_even[e].astype(BF16), xs, gate1)
        else:
            o = layer // 2
            lam_init = 0.8 - 0.6 * math.exp(-0.3 * layer)
            lam = (jnp.exp(jnp.sum(lam_q1[o].astype(F32) * lam_k1[o].astype(F32)))
                   - jnp.exp(jnp.sum(lam_q2[o].astype(F32) * lam_k2[o].astype(F32))) + lam_init)
            qkv = norm_mod_proj_rope(xs, norm1[layer], mod1, w_qkv[o].astype(BF16), cos_tab, sin_tab)
            a = diff_attention(qkv, lam, subln[o], 1.0 - lam_init)
            xs = out_proj_residual(a, w_out_odd[o].astype(BF16), xs, gate1)
        h2, q = norm_mod_proj_h(xs, norm2[layer], mod2, peer_wq[layer].astype(BF16))
        f = _peer_from_query(h2.reshape(BATCH * SEQ_ALL, D_MODEL), q.reshape(BATCH * SEQ_ALL, -1),
                             peer_keys[layer], peer_u[layer], peer_v[layer])
        g2 = jnp.concatenate([jnp.broadcast_to(gate2[:, 0], (BATCH, CTX_LEN, D_MODEL)),
                              jnp.broadcast_to(gate2[:, 1], (BATCH, SEQ, D_MODEL))], axis=1)
        xs = xs + g2 * f.reshape(BATCH, SEQ_ALL, D_MODEL)
    return _rmsnorm(xs[:, CTX_LEN:], final_norm)
```

```python
import functools
import math

import jax
import jax.numpy as jnp
from jax import lax
from jax.experimental import pallas as pl
from jax.experimental.pallas import tpu as pltpu

F32 = jnp.float32
BF16 = jnp.bfloat16

D_MODEL = 1024
BATCH = 8
SEQ = 4096
DEPTH = 4
GRID_W = 64
CTX_LEN = 256
SEQ_ALL = CTX_LEN + SEQ
ROW_BLOCK = CTX_LEN
N_ROW_BLOCKS = SEQ_ALL // ROW_BLOCK

GDN_HEAD_DIM = 128
GDN_WIDTH = D_MODEL // 2
GDN_HEADS = GDN_WIDTH // GDN_HEAD_DIM
GDN_CHUNK = 64
SHORT_CONV = 3

HYENA_WIDTH = D_MODEL // 2
HYENA_ORDER = 2
HYENA_EMB = 33
HYENA_DECAY_TARGET = 1e-2
HYENA_MIN_DECAY = math.log(HYENA_DECAY_TARGET) / 1.5
HYENA_MAX_DECAY = math.log(HYENA_DECAY_TARGET) / 0.3

N_CONV = 3 * GDN_WIDTH + 3 * HYENA_WIDTH
N_IN = N_CONV + GDN_WIDTH + 4 * GDN_HEADS
LANES = 128
N_IN_PAD = -(-N_IN // LANES) * LANES

DIFF_HEAD_DIM = 64
DIFF_HEADS = D_MODEL // (2 * DIFF_HEAD_DIM)
DIFF_V_DIM = 2 * DIFF_HEAD_DIM
DIFF_QK_WIDTH = DIFF_HEADS * 2 * DIFF_HEAD_DIM
DIFF_V_WIDTH = DIFF_HEADS * DIFF_V_DIM
ROPE_BASE = 10000.0
ROPE_FREQS = DIFF_HEAD_DIM // 4

PEER_HEADS = 8
PEER_KEYS = 128
PEER_TOPK = 16
PEER_QDIM = 256
PEER_TOKEN_BLOCK = 128
PEER_PICKS = PEER_HEADS * PEER_TOPK
N_EXPERTS = PEER_KEYS * PEER_KEYS
PEER_TOKENS_PER_STEP = 64
NORM_EPS = 1e-6

VMEM_LIMIT = 56 * 1024 * 1024


def _cparams(*sem):
    return pltpu.CompilerParams(dimension_semantics=sem, vmem_limit_bytes=VMEM_LIMIT)


def _norm_mod(x, g, mod):
    y = x * lax.rsqrt(jnp.mean(x * x, axis=-1, keepdims=True) + NORM_EPS) * g
    return y * (1.0 + mod[1:2, :]) + mod[0:1, :]


def _proj_kernel(x_ref, g_ref, mod_ref, w_ref, o_ref):
    h = _norm_mod(x_ref[0], g_ref[...], mod_ref[0, 0])
    o_ref[0] = jnp.dot(h.astype(BF16), w_ref[...], preferred_element_type=F32).astype(o_ref.dtype)


def _proj_h_kernel(x_ref, g_ref, mod_ref, w_ref, h_ref, o_ref):
    h = _norm_mod(x_ref[0], g_ref[...], mod_ref[0, 0])
    h_ref[0] = h
    o_ref[0] = jnp.dot(h.astype(BF16), w_ref[...], preferred_element_type=F32)


def _proj_rope_kernel(x_ref, g_ref, mod_ref, w_ref, cos_ref, sin_ref, o_ref):
    h = _norm_mod(x_ref[0], g_ref[...], mod_ref[0, 0])
    p = jnp.dot(h.astype(BF16), w_ref[...], preferred_element_type=F32)
    cos = cos_ref[...]
    sin = sin_ref[...]
    lane = lax.broadcasted_iota(jnp.int32, (1, LANES), 1)
    first_half = (lane % (2 * ROPE_FREQS)) < ROPE_FREQS
    n_rot = 2 * DIFF_QK_WIDTH // LANES
    for j in range(p.shape[1] // LANES):
        blk = p[:, j * LANES:(j + 1) * LANES]
        if j < n_rot:
            partner = jnp.where(first_half, pltpu.roll(blk, LANES - ROPE_FREQS, 1),
                                pltpu.roll(blk, ROPE_FREQS, 1))
            blk = blk * cos + partner * sin
            if j < n_rot // 2:
                blk = blk * (DIFF_HEAD_DIM ** -0.5)
        o_ref[0, :, j * LANES:(j + 1) * LANES] = blk.astype(o_ref.dtype)


def _row_specs(n_out):
    x_spec = pl.BlockSpec((1, ROW_BLOCK, D_MODEL), lambda b, s: (b, s, 0))
    g_spec = pl.BlockSpec((1, D_MODEL), lambda b, s: (0, 0))
    mod_spec = pl.BlockSpec((1, 1, 2, D_MODEL), lambda b, s: (b, jnp.minimum(s, 1), 0, 0))
    w_spec = pl.BlockSpec((D_MODEL, n_out), lambda b, s: (0, 0))
    o_spec = pl.BlockSpec((1, ROW_BLOCK, n_out), lambda b, s: (b, s, 0))
    return x_spec, g_spec, mod_spec, w_spec, o_spec


def norm_mod_proj(x, g, mod, w, out_dtype=F32):
    n_out = w.shape[1]
    x_spec, g_spec, mod_spec, w_spec, o_spec = _row_specs(n_out)
    return pl.pallas_call(
        _proj_kernel, grid=(BATCH, N_ROW_BLOCKS),
        in_specs=[x_spec, g_spec, mod_spec, w_spec], out_specs=o_spec,
        out_shape=jax.ShapeDtypeStruct((BATCH, SEQ_ALL, n_out), out_dtype),
        compiler_params=_cparams("parallel", "parallel"), name="norm_mod_proj",
    )(x, g.reshape(1, D_MODEL), mod, w)


def norm_mod_proj_h(x, g, mod, w):
    n_out = w.shape[1]
    x_spec, g_spec, mod_spec, w_spec, o_spec = _row_specs(n_out)
    return pl.pallas_call(
        _proj_h_kernel, grid=(BATCH, N_ROW_BLOCKS),
        in_specs=[x_spec, g_spec, mod_spec, w_spec], out_specs=[x_spec, o_spec],
        out_shape=[jax.ShapeDtypeStruct((BATCH, SEQ_ALL, D_MODEL), F32),
                   jax.ShapeDtypeStruct((BATCH, SEQ_ALL, n_out), F32)],
        compiler_params=_cparams("parallel", "parallel"), name="norm_mod_proj_h",
    )(x, g.reshape(1, D_MODEL), mod, w)


def norm_mod_proj_rope(x, g, mod, w, cos_tab, sin_tab):
    n_out = w.shape[1]
    x_spec, g_spec, mod_spec, w_spec, o_spec = _row_specs(n_out)
    tab_spec = pl.BlockSpec((ROW_BLOCK, LANES), lambda b, s: (s, 0))
    return pl.pallas_call(
        _proj_rope_kernel, grid=(BATCH, N_ROW_BLOCKS),
        in_specs=[x_spec, g_spec, mod_spec, w_spec, tab_spec, tab_spec], out_specs=o_spec,
        out_shape=jax.ShapeDtypeStruct((BATCH, SEQ_ALL, n_out), BF16),
        compiler_params=_cparams("parallel", "parallel"), name="norm_mod_proj_rope",
    )(x, g.reshape(1, D_MODEL), mod, w, cos_tab, sin_tab)


def _out_proj_kernel(a_ref, w_ref, x_ref, gate_ref, o_ref):
    y = jnp.dot(a_ref[0].astype(BF16), w_ref[...], preferred_element_type=F32)
    o_ref[0] = x_ref[0] + gate_ref[0, 0] * y


def out_proj_residual(a, w, x, gate):
    k = a.shape[-1]
    return pl.pallas_call(
        _out_proj_kernel, grid=(BATCH, N_ROW_BLOCKS),
        in_specs=[pl.BlockSpec((1, ROW_BLOCK, k), lambda b, s: (b, s, 0)),
                  pl.BlockSpec((k, D_MODEL), lambda b, s: (0, 0)),
                  pl.BlockSpec((1, ROW_BLOCK, D_MODEL), lambda b, s: (b, s, 0)),
                  pl.BlockSpec((1, 1, 1, D_MODEL), lambda b, s: (b, jnp.minimum(s, 1), 0, 0))],
        out_specs=pl.BlockSpec((1, ROW_BLOCK, D_MODEL), lambda b, s: (b, s, 0)),
        out_shape=jax.ShapeDtypeStruct((BATCH, SEQ_ALL, D_MODEL), F32),
        compiler_params=_cparams("parallel", "parallel"), name="out_proj_residual",
    )(a, w, x, gate)


def _diff_attn_kernel(lam_ref, q_ref, k_ref, v_ref, gain_ref, o_ref, *, out_scale):
    lam = lam_ref[0]
    lane = lax.broadcasted_iota(jnp.int32, (1, LANES), 1)
    in_map0 = lane < DIFF_HEAD_DIM

    def attend(n_keys):
        q = q_ref[0]
        k = k_ref[0, :n_keys, :]
        v = v_ref[0, :n_keys, :]
        zero = jnp.zeros_like(q)
        nt = (((1,), (1,)), ((), ()))
        s0 = lax.dot_general(jnp.where(in_map0, q, zero), k, nt, preferred_element_type=F32)
        s1 = lax.dot_general(jnp.where(in_map0, zero, q), k, nt, preferred_element_type=F32)
        p0 = jnp.exp(s0 - jnp.max(s0, axis=-1, keepdims=True))
        p1 = jnp.exp(s1 - jnp.max(s1, axis=-1, keepdims=True))
        r0 = 1.0 / jnp.sum(p0, axis=-1, keepdims=True)
        r1 = lam / jnp.sum(p1, axis=-1, keepdims=True)
        a = p0 * r0 - p1 * r1
        o = jnp.dot(a.astype(BF16), v, preferred_element_type=F32)
        o = o * lax.rsqrt(jnp.mean(o * o, axis=-1, keepdims=True) + NORM_EPS)
        o_ref[0] = (o * gain_ref[...] * out_scale).astype(o_ref.dtype)

    is_ctx = pl.program_id(2) == 0

    @pl.when(is_ctx)
    def _():
        attend(CTX_LEN)

    @pl.when(jnp.logical_not(is_ctx))
    def _():
        attend(SEQ_ALL)


def diff_attention(qkv, lam, gain, out_scale):
    kv_blocks = DIFF_QK_WIDTH // LANES
    return pl.pallas_call(
        functools.partial(_diff_attn_kernel, out_scale=out_scale),
        grid=(BATCH, DIFF_HEADS, N_ROW_BLOCKS),
        in_specs=[pl.BlockSpec(memory_space=pltpu.SMEM),
                  pl.BlockSpec((1, ROW_BLOCK, LANES), lambda b, h, s: (b, s, h)),
                  pl.BlockSpec((1, SEQ_ALL, LANES), lambda b, h, s: (b, 0, kv_blocks + h)),
                  pl.BlockSpec((1, SEQ_ALL, LANES), lambda b, h, s: (b, 0, 2 * kv_blocks + h)),
                  pl.BlockSpec((1, LANES), lambda b, h, s: (0, 0))],
        out_specs=pl.BlockSpec((1, ROW_BLOCK, LANES), lambda b, h, s: (b, s, h)),
        out_shape=jax.ShapeDtypeStruct((BATCH, SEQ_ALL, DIFF_V_WIDTH), BF16),
        compiler_params=_cparams("parallel", "parallel", "arbitrary"), name="diff_attention",
    )(lam.reshape(1), qkv, qkv, qkv, gain.reshape(1, LANES))


def rope_tables():
    t = jnp.arange(SEQ)
    pos = jnp.stack([t // GRID_W, t % GRID_W], axis=-1).astype(F32)
    inv = ROPE_BASE ** (-jnp.arange(ROPE_FREQS, dtype=F32) / ROPE_FREQS)
    ang = pos[:, :, None] * inv
    cos = jnp.cos(ang)
    sin = jnp.sin(ang)
    cos_l = jnp.tile(jnp.concatenate([cos, cos], axis=-1).reshape(SEQ, 4 * ROPE_FREQS), (1, 2))
    sin_l = jnp.tile(jnp.concatenate([-sin, sin], axis=-1).reshape(SEQ, 4 * ROPE_FREQS), (1, 2))
    cos_t = jnp.concatenate([jnp.ones((CTX_LEN, LANES), F32), cos_l], axis=0)
    sin_t = jnp.concatenate([jnp.zeros((CTX_LEN, LANES), F32), sin_l], axis=0)
    return cos_t, sin_t


def _rmsnorm(x, g):
    xf = x.astype(F32)
    y = xf * lax.rsqrt(jnp.mean(xf * xf, axis=-1, keepdims=True) + NORM_EPS)
    return (y * g.astype(F32)).astype(x.dtype)


def _l2norm(x):
    x = x.astype(F32)
    return x * lax.rsqrt(jnp.sum(x * x, axis=-1, keepdims=True) + NORM_EPS)


def _depthwise_conv(x, w):
    return lax.conv_general_dilated(x, w[:, None, :].astype(x.dtype), window_strides=(1,),
                                    padding=[(SHORT_CONV // 2, SHORT_CONV // 2)],
                                    dimension_numbers=('NWC', 'WIO', 'NWC'),
                                    feature_group_count=x.shape[-1])


def _gated_delta_chunked(q, k, v, beta, g, s0):
    B, H, L, dk = q.shape
    dv = v.shape[-1]
    C = GDN_CHUNK
    n = L // C
    hp = lax.Precision.HIGHEST
    q, k, v, beta, g = [t.astype(F32).reshape(B, H, n, C, *t.shape[3:]) for t in (q, k, v, beta, g)]
    G = jnp.cumsum(g, axis=-1)
    causal = jnp.tril(jnp.ones((C, C), bool))
    strict = jnp.tril(jnp.ones((C, C), bool), -1)
    gdiff = G[..., :, None] - G[..., None, :]
    decay = jnp.where(causal, jnp.exp(jnp.where(causal, gdiff, 0.0)), 0.0)
    kk = jnp.einsum('bhncd,bhnjd->bhncj', k, k, precision=hp)
    a_mat = jnp.where(strict, beta[..., :, None] * kk * decay, 0.0)
    eye = jnp.eye(C, dtype=F32)
    rhs = jnp.concatenate([beta[..., None] * v, (beta * jnp.exp(G))[..., None] * k], axis=-1)
    sol = lax.linalg.triangular_solve(eye + a_mat, rhs, left_side=True, lower=True, unit_diagonal=True)
    w_intra, k_cum = sol[..., :dv], sol[..., dv:]
    qk = jnp.einsum('bhncd,bhnjd->bhncj', q, k, precision=hp) * decay
    q_dec = q * jnp.exp(G)[..., None]
    k_dec = k * jnp.exp(G[..., -1:] - G)[..., None]
    chunk_decay = jnp.exp(G[..., -1])

    def step(s, inp):
        w_i, kc_i, qk_i, qd_i, kd_i, cd_i = inp
        w = w_i - jnp.einsum('bhcd,bhde->bhce', kc_i, s, precision=hp)
        o = (jnp.einsum('bhcd,bhde->bhce', qd_i, s, precision=hp)
             + jnp.einsum('bhcj,bhje->bhce', qk_i, w, precision=hp))
        s = s * cd_i[..., None, None] + jnp.einsum('bhcd,bhce->bhde', kd_i, w, precision=hp)
        return s, o

    xs = tuple(jnp.moveaxis(t, 2, 0) for t in (w_intra, k_cum, qk, q_dec, k_dec, chunk_decay))
    s_fin, o = lax.scan(step, s0.astype(F32), xs)
    o = jnp.moveaxis(o, 0, 2).reshape(B, H, L, dv)
    return o, s_fin


def _gdn_inputs(qkv, ba, a_log, dt_bias):
    B, L, _ = qkv.shape
    qkv = jnp.transpose(jax.nn.silu(qkv).reshape(B, L, 3, GDN_HEADS, GDN_HEAD_DIM), (2, 0, 3, 1, 4))
    q = _l2norm(qkv[0]) * (GDN_HEAD_DIM ** -0.5)
    k = _l2norm(qkv[1])
    v = qkv[2]
    ba = ba.astype(F32).reshape(B, L, 2, 2, GDN_HEADS)
    beta = jax.nn.sigmoid(ba[:, :, 0])
    g = -jnp.exp(a_log.astype(F32)) * jax.nn.softplus(ba[:, :, 1] + dt_bias.astype(F32))
    return q, k, v, jnp.transpose(beta, (2, 0, 3, 1)), jnp.transpose(g, (2, 0, 3, 1))


def _gdn_bidir(q, k, v, beta, g, s0):
    o_f, s_f = _gated_delta_chunked(q, k, v, beta[0], g[0], s0[0])
    fl = lambda t: jnp.flip(t, axis=2)
    o_b, s_b = _gated_delta_chunked(fl(q), fl(k), fl(v), fl(beta[1]), fl(g[1]), s0[1])
    return o_f + fl(o_b), jnp.stack([s_f, s_b])


def _hyena_filters(L, w1, b1, w2, b2, w3, b3, w4, freq):
    hp = lax.Precision.HIGHEST
    t = jnp.linspace(0.0, 1.0, L, dtype=F32)
    bands = (HYENA_EMB - 1) // 2
    wpos = 2.0 * math.pi * jnp.arange(L, dtype=F32) / L
    fb = jnp.linspace(1e-4, bands - 1, bands, dtype=F32)
    z = jnp.concatenate([t[:, None], jnp.cos(wpos[:, None] * fb), -jnp.sin(wpos[:, None] * fb)], axis=-1)
    h = jnp.sin(freq * (jnp.dot(z, w1, precision=hp) + b1))
    h = jnp.sin(freq * (jnp.dot(h, w2, precision=hp) + b2))
    h = jnp.sin(freq * (jnp.dot(h, w3, precision=hp) + b3))
    h = jnp.dot(h, w4, precision=hp).astype(F32).reshape(L, HYENA_ORDER, 2, HYENA_WIDTH)
    deltas = jnp.linspace(HYENA_MIN_DECAY, HYENA_MAX_DECAY, HYENA_WIDTH, dtype=F32)
    window = jnp.exp(-t[:, None] * jnp.abs(deltas))
    return h * window[:, None, None, :]


def _two_sided_long_conv(u, h_fwd, h_bwd, bias):
    B, L, C = u.shape
    filt2 = jnp.concatenate([h_fwd, jnp.zeros((1, C), F32), jnp.flip(h_bwd[1:], axis=0)], axis=0)
    uf = jnp.fft.rfft(u.astype(F32), n=2 * L, axis=1)
    ff = jnp.fft.rfft(filt2, n=2 * L, axis=0)
    y = jnp.fft.irfft(uf * ff[None], n=2 * L, axis=1)[:, :L]
    return (y + u.astype(F32) * bias.astype(F32)).astype(u.dtype)


def _hyena(xh, filt, bias):
    z = xh[:, :, 2]
    for o in range(HYENA_ORDER):
        z = xh[:, :, o] * _two_sided_long_conv(z, filt[:, o, 0], filt[:, o, 1], bias[o])
    return z


def _even_mixer_core(p, conv_w, a_log, dt_bias, gdn_gain, hf, hy_bias):
    def split(pp):
        B, L, _ = pp.shape
        cv = _depthwise_conv(pp[..., :N_CONV], conv_w)
        return (cv[..., :3 * GDN_WIDTH], cv[..., 3 * GDN_WIDTH:].reshape(B, L, 3, HYENA_WIDTH),
                pp[..., N_CONV:N_CONV + GDN_WIDTH], pp[..., N_CONV + GDN_WIDTH:])

    def gdn_out(o, z):
        B, H, L, dv = o.shape
        o = jnp.transpose(o, (0, 2, 1, 3)).astype(z.dtype)
        o = _rmsnorm(o, gdn_gain) * jax.nn.silu(z.reshape(B, L, H, dv))
        return o.reshape(B, L, H * dv)

    def hyena_branch(xh):
        return _hyena(xh, _hyena_filters(xh.shape[1], *hf), hy_bias)

    B = p.shape[0]
    qkv_c, hy_c, z_c, ba_c = split(p[:, :CTX_LEN])
    q, k, v, beta, g = _gdn_inputs(qkv_c, ba_c, a_log, dt_bias)
    s0 = jnp.zeros((2, B, GDN_HEADS, GDN_HEAD_DIM, GDN_HEAD_DIM), F32)
    o_c, s_c = _gdn_bidir(q, k, v, beta, g, s0)
    qkv_l, hy_l, z_l, ba_l = split(p[:, CTX_LEN:])
    q, k, v, beta, g = _gdn_inputs(qkv_l, ba_l, a_log, dt_bias)
    o_l, _ = _gdn_bidir(q, k, v, beta, g, s_c)
    a_lat = jnp.concatenate([gdn_out(o_l, z_l), hyena_branch(hy_l)], axis=-1)
    a_ctx = jnp.concatenate([gdn_out(o_c, z_c), hyena_branch(hy_c)], axis=-1)
    return jnp.concatenate([a_ctx, a_lat], axis=1)


def _peer_select(q, keys):
    T = q.shape[0]
    nb = T // PEER_TOKEN_BLOCK
    qb_all = q.reshape(nb, PEER_TOKEN_BLOCK, PEER_HEADS, 2, PEER_QDIM // 2)

    def one(qb):
        t = qb.shape[0]
        s = jnp.einsum('thps,hpns->thpn', qb, keys, precision=lax.Precision.HIGHEST).astype(F32)
        sv, si = lax.top_k(s, PEER_TOPK)
        cand = (sv[:, :, 0, :, None] + sv[:, :, 1, None, :]).reshape(t, PEER_HEADS, PEER_TOPK * PEER_TOPK)
        cidx = (si[:, :, 0, :, None] * PEER_KEYS + si[:, :, 1, None, :]).reshape(t, PEER_HEADS, PEER_TOPK * PEER_TOPK)
        top_s, pos = lax.top_k(cand, PEER_TOPK)
        idx = jnp.take_along_axis(cidx, pos, axis=-1)
        gate = jax.nn.softmax(top_s, axis=-1)
        return idx.reshape(t, PEER_PICKS).astype(jnp.int32), gate.reshape(t, PEER_PICKS)

    idx, gate = lax.map(one, qb_all)
    return idx.reshape(T, PEER_PICKS), gate.reshape(T, PEER_PICKS)


def pack_expert_table(tab):
    bits = lax.bitcast_convert_type(tab.astype(BF16), jnp.uint16).astype(jnp.uint32)
    bits = bits.reshape(N_EXPERTS // 2, 2, D_MODEL // LANES, LANES)
    return bits[:, 0] | (bits[:, 1] << 16)


def _expert_row(tab_ref, tile, shift):
    w = tab_ref[tile]
    return lax.bitcast_convert_type((w << shift.astype(jnp.uint32)) & jnp.uint32(0xFFFF0000), F32)


def _peer_act_kernel(tile_ref, shift_ref, h_ref, tab_ref, act_ref):
    lane = lax.broadcasted_iota(jnp.int32, (D_MODEL // LANES, LANES), 1)

    def token(t, carry):
        h = h_ref[t]
        acc = jnp.zeros((D_MODEL // LANES, LANES), F32)
        for p in range(PEER_PICKS):
            row = _expert_row(tab_ref, tile_ref[t, p], shift_ref[t, p])
            acc = jnp.where(lane == p, jnp.sum(row * h, axis=1, keepdims=True), acc)
        act_ref[pl.ds(t, 1), :] = jnp.sum(acc, axis=0, keepdims=True)
        return carry

    lax.fori_loop(0, PEER_TOKENS_PER_STEP, token, 0)


def _peer_out_kernel(tile_ref, shift_ref, coef_ref, x_ref, gate_ref, tab_ref, o_ref):
    n_acc = 4

    def token(t, carry):
        accs = [jnp.zeros((D_MODEL // LANES, LANES), F32) for _ in range(n_acc)]
        for p in range(PEER_PICKS):
            row = _expert_row(tab_ref, tile_ref[t, p], shift_ref[t, p])
            accs[p % n_acc] = accs[p % n_acc] + coef_ref[t, p] * row
        f = (accs[0] + accs[1]) + (accs[2] + accs[3])
        o_ref[t] = x_ref[t] + gate_ref[0, 0] * f
        return carry

    lax.fori_loop(0, PEER_TOKENS_PER_STEP, token, 0)


def _peer_specs():
    n_tok = BATCH * SEQ_ALL
    steps_per_batch = SEQ_ALL // PEER_TOKENS_PER_STEP
    ctx_steps = CTX_LEN // PEER_TOKENS_PER_STEP
    rows = D_MODEL // LANES
    smem = pl.BlockSpec((PEER_TOKENS_PER_STEP, PEER_PICKS), lambda i: (i, 0), memory_space=pltpu.SMEM)
    tok = pl.BlockSpec((PEER_TOKENS_PER_STEP, rows, LANES), lambda i: (i, 0, 0))
    tab = pl.BlockSpec((N_EXPERTS // 2, rows, LANES), lambda i: (0, 0, 0), pipeline_mode=pl.Buffered(1))
    gate = pl.BlockSpec((1, 1, rows, LANES),
                        lambda i: (i // steps_per_batch, jnp.minimum((i % steps_per_batch) // ctx_steps, 1), 0, 0))
    picks = pl.BlockSpec((PEER_TOKENS_PER_STEP, PEER_PICKS), lambda i: (i, 0))
    return n_tok, rows, smem, tok, tab, gate, picks


def peer_activations(tile, shift, h, tab):
    n_tok, rows, smem, tok, tabspec, _, picks = _peer_specs()
    return pl.pallas_call(
        _peer_act_kernel, grid=(n_tok // PEER_TOKENS_PER_STEP,),
        in_specs=[smem, smem, tok, tabspec], out_specs=picks,
        out_shape=jax.ShapeDtypeStruct((n_tok, PEER_PICKS), F32),
        compiler_params=_cparams("parallel"), name="peer_activations",
    )(tile, shift, h.reshape(n_tok, rows, LANES), tab)


def peer_output_residual(tile, shift, coef, x, gate, tab):
    n_tok, rows, smem, tok, tabspec, gatespec, _ = _peer_specs()
    out = pl.pallas_call(
        _peer_out_kernel, grid=(n_tok // PEER_TOKENS_PER_STEP,),
        in_specs=[smem, smem, smem, tok, gatespec, tabspec], out_specs=tok,
        out_shape=jax.ShapeDtypeStruct((n_tok, rows, LANES), F32),
        compiler_params=_cparams("parallel"), name="peer_output_residual",
    )(tile, shift, coef, x.reshape(n_tok, rows, LANES), gate.reshape(BATCH, 2, rows, LANES), tab)
    return out.reshape(BATCH, SEQ_ALL, D_MODEL)


def kernel(x, c, ctx, c_ctx, norm1, norm2, w_ada, b_ada, w_in, conv_w, a_log, dt_bias, gdn_gain, hf_w1, hf_b1, hf_w2, hf_b2, hf_w3, hf_b3, hf_w4, hf_freq, hy_bias, w_out_even, w_qkv, lam_q1, lam_k1, lam_q2, lam_k2, subln, w_out_odd, peer_wq, peer_keys, peer_u, peer_v, final_norm):
    hp = lax.Precision.HIGHEST
    xs = jnp.concatenate([ctx, x], axis=1)
    cos_tab, sin_tab = rope_tables()
    cond = jnp.concatenate([jax.nn.silu(c_ctx)[None], jax.nn.silu(c)], axis=0)
    for layer in range(DEPTH):
        m = (jnp.dot(cond, w_ada[layer], precision=hp) + b_ada[layer]).reshape(BATCH + 1, 6, D_MODEL)
        m = jnp.stack([jnp.broadcast_to(m[0], (BATCH, 6, D_MODEL)), m[1:]], axis=1)
        mod1, gate1 = m[:, :, 0:2], m[:, :, 2:3]
        mod2, gate2 = m[:, :, 3:5], m[:, :, 5:6]
        if layer % 2 == 0:
            e = layer // 2
            w_pad = jnp.pad(w_in[e], ((0, 0), (0, N_IN_PAD - N_IN))).astype(BF16)
            p = norm_mod_proj(xs, norm1[layer], mod1, w_pad)[..., :N_IN]
            hf = (hf_w1[e], hf_b1[e], hf_w2[e], hf_b2[e], hf_w3[e], hf_b3[e], hf_w4[e], hf_freq[e])
            a = _even_mixer_core(p, conv_w[e], a_log[e], dt_bias[e], gdn_gain[e], hf, hy_bias[e])
            xs = out_proj_residual(a, w_out_even[e].astype(BF16), xs, gate1)
        else:
            o = layer // 2
            lam_init = 0.8 - 0.6 * math.exp(-0.3 * layer)
            lam = (jnp.exp(jnp.sum(lam_q1[o].astype(F32) * lam_k1[o].astype(F32)))
                   - jnp.exp(jnp.sum(lam_q2[o].astype(F32) * lam_k2[o].astype(F32))) + lam_init)
            qkv = norm_mod_proj_rope(xs, norm1[layer], mod1, w_qkv[o].astype(BF16), cos_tab, sin_tab)
            a = diff_attention(qkv, lam, subln[o], 1.0 - lam_init)
            xs = out_proj_residual(a, w_out_odd[o].astype(BF16), xs, gate1)
        h2, q = norm_mod_proj_h(xs, norm2[layer], mod2, peer_wq[layer].astype(BF16))
        idx, gate = _peer_select(q.reshape(BATCH * SEQ_ALL, -1), peer_keys[layer])
        tile = idx >> 1
        shift = 16 - 16 * (idx & 1)
        act = peer_activations(tile, shift, h2, pack_expert_table(peer_u[layer]))
        coef = gate * jax.nn.gelu(act, approximate=False)
        xs = peer_output_residual(tile, shift, coef, xs, gate2, pack_expert_table(peer_v[layer]))
    return _rmsnorm(xs[:, CTX_LEN:], final_norm)
```

```python
import functools
import math

import jax
import jax.numpy as jnp
from jax import lax
from jax.experimental import pallas as pl
from jax.experimental.pallas import tpu as pltpu

F32 = jnp.float32
BF16 = jnp.bfloat16

D_MODEL = 1024
BATCH = 8
SEQ = 4096
DEPTH = 4
GRID_W = 64
CTX_LEN = 256
SEQ_ALL = CTX_LEN + SEQ
ROW_BLOCK = CTX_LEN
N_ROW_BLOCKS = SEQ_ALL // ROW_BLOCK

GDN_HEAD_DIM = 128
GDN_WIDTH = D_MODEL // 2
GDN_HEADS = GDN_WIDTH // GDN_HEAD_DIM
GDN_CHUNK = 64
SHORT_CONV = 3

HYENA_WIDTH = D_MODEL // 2
HYENA_ORDER = 2
HYENA_EMB = 33
HYENA_DECAY_TARGET = 1e-2
HYENA_MIN_DECAY = math.log(HYENA_DECAY_TARGET) / 1.5
HYENA_MAX_DECAY = math.log(HYENA_DECAY_TARGET) / 0.3

N_CONV = 3 * GDN_WIDTH + 3 * HYENA_WIDTH
N_IN = N_CONV + GDN_WIDTH + 4 * GDN_HEADS
LANES = 128
N_IN_PAD = -(-N_IN // LANES) * LANES

DIFF_HEAD_DIM = 64
DIFF_HEADS = D_MODEL // (2 * DIFF_HEAD_DIM)
DIFF_V_DIM = 2 * DIFF_HEAD_DIM
DIFF_QK_WIDTH = DIFF_HEADS * 2 * DIFF_HEAD_DIM
DIFF_V_WIDTH = DIFF_HEADS * DIFF_V_DIM
ROPE_BASE = 10000.0
ROPE_FREQS = DIFF_HEAD_DIM // 4

PEER_HEADS = 8
PEER_KEYS = 128
PEER_TOPK = 16
PEER_QDIM = 256
PEER_TOKEN_BLOCK = 128
PEER_PICKS = PEER_HEADS * PEER_TOPK
N_EXPERTS = PEER_KEYS * PEER_KEYS
PEER_TOKENS_PER_STEP = 128
ROW_TILE = 8
NORM_EPS = 1e-6

VMEM_LIMIT = 56 * 1024 * 1024


def _cparams(*sem):
    return pltpu.CompilerParams(dimension_semantics=sem, vmem_limit_bytes=VMEM_LIMIT)


def _norm_mod(x, g, mod):
    y = x * lax.rsqrt(jnp.mean(x * x, axis=-1, keepdims=True) + NORM_EPS) * g
    return y * (1.0 + mod[1:2, :]) + mod[0:1, :]


def _proj_kernel(x_ref, g_ref, mod_ref, w_ref, o_ref):
    h = _norm_mod(x_ref[0], g_ref[...], mod_ref[0, 0])
    o_ref[0] = jnp.dot(h.astype(BF16), w_ref[...], preferred_element_type=F32).astype(o_ref.dtype)


def _proj_h_kernel(x_ref, g_ref, mod_ref, w_ref, h_ref, o_ref):
    h = _norm_mod(x_ref[0], g_ref[...], mod_ref[0, 0])
    h_ref[0] = h
    o_ref[0] = jnp.dot(h.astype(BF16), w_ref[...], preferred_element_type=F32)


def _proj_rope_kernel(x_ref, g_ref, mod_ref, w_ref, cos_ref, sin_ref, o_ref):
    h = _norm_mod(x_ref[0], g_ref[...], mod_ref[0, 0])
    p = jnp.dot(h.astype(BF16), w_ref[...], preferred_element_type=F32)
    cos = cos_ref[...]
    sin = sin_ref[...]
    lane = lax.broadcasted_iota(jnp.int32, (1, LANES), 1)
    first_half = (lane % (2 * ROPE_FREQS)) < ROPE_FREQS
    n_rot = 2 * DIFF_QK_WIDTH // LANES
    for j in range(p.shape[1] // LANES):
        blk = p[:, j * LANES:(j + 1) * LANES]
        if j < n_rot:
            partner = jnp.where(first_half, pltpu.roll(blk, LANES - ROPE_FREQS, 1),
                                pltpu.roll(blk, ROPE_FREQS, 1))
            blk = blk * cos + partner * sin
            if j < n_rot // 2:
                blk = blk * (DIFF_HEAD_DIM ** -0.5)
        o_ref[0, :, j * LANES:(j + 1) * LANES] = blk.astype(o_ref.dtype)


def _row_specs(n_out):
    x_spec = pl.BlockSpec((1, ROW_BLOCK, D_MODEL), lambda b, s: (b, s, 0))
    g_spec = pl.BlockSpec((1, D_MODEL), lambda b, s: (0, 0))
    mod_spec = pl.BlockSpec((1, 1, 2, D_MODEL), lambda b, s: (b, jnp.minimum(s, 1), 0, 0))
    w_spec = pl.BlockSpec((D_MODEL, n_out), lambda b, s: (0, 0))
    o_spec = pl.BlockSpec((1, ROW_BLOCK, n_out), lambda b, s: (b, s, 0))
    return x_spec, g_spec, mod_spec, w_spec, o_spec


def norm_mod_proj(x, g, mod, w, out_dtype=F32):
    n_out = w.shape[1]
    x_spec, g_spec, mod_spec, w_spec, o_spec = _row_specs(n_out)
    return pl.pallas_call(
        _proj_kernel, grid=(BATCH, N_ROW_BLOCKS),
        in_specs=[x_spec, g_spec, mod_spec, w_spec], out_specs=o_spec,
        out_shape=jax.ShapeDtypeStruct((BATCH, SEQ_ALL, n_out), out_dtype),
        compiler_params=_cparams("parallel", "parallel"), name="norm_mod_proj",
    )(x, g.reshape(1, D_MODEL), mod, w)


def norm_mod_proj_h(x, g, mod, w):
    n_out = w.shape[1]
    x_spec, g_spec, mod_spec, w_spec, o_spec = _row_specs(n_out)
    return pl.pallas_call(
        _proj_h_kernel, grid=(BATCH, N_ROW_BLOCKS),
        in_specs=[x_spec, g_spec, mod_spec, w_spec], out_specs=[x_spec, o_spec],
        out_shape=[jax.ShapeDtypeStruct((BATCH, SEQ_ALL, D_MODEL), F32),
                   jax.ShapeDtypeStruct((BATCH, SEQ_ALL, n_out), F32)],
        compiler_params=_cparams("parallel", "parallel"), name="norm_mod_proj_h",
    )(x, g.reshape(1, D_MODEL), mod, w)


def norm_mod_proj_rope(x, g, mod, w, cos_tab, sin_tab):
    n_out = w.shape[1]
    x_spec, g_spec, mod_spec, w_spec, o_spec = _row_specs(n_out)
    tab_spec = pl.BlockSpec((ROW_BLOCK, LANES), lambda b, s: (s, 0))
    return pl.pallas_call(
        _proj_rope_kernel, grid=(BATCH, N_ROW_BLOCKS),
        in_specs=[x_spec, g_spec, mod_spec, w_spec, tab_spec, tab_spec], out_specs=o_spec,
        out_shape=jax.ShapeDtypeStruct((BATCH, SEQ_ALL, n_out), BF16),
        compiler_params=_cparams("parallel", "parallel"), name="norm_mod_proj_rope",
    )(x, g.reshape(1, D_MODEL), mod, w, cos_tab, sin_tab)


def _out_proj_kernel(a_ref, w_ref, x_ref, gate_ref, o_ref):
    y = jnp.dot(a_ref[0].astype(BF16), w_ref[...], preferred_element_type=F32)
    o_ref[0] = x_ref[0] + gate_ref[0, 0] * y


def out_proj_residual(a, w, x, gate):
    k = a.shape[-1]
    return pl.pallas_call(
        _out_proj_kernel, grid=(BATCH, N_ROW_BLOCKS),
        in_specs=[pl.BlockSpec((1, ROW_BLOCK, k), lambda b, s: (b, s, 0)),
                  pl.BlockSpec((k, D_MODEL), lambda b, s: (0, 0)),
                  pl.BlockSpec((1, ROW_BLOCK, D_MODEL), lambda b, s: (b, s, 0)),
                  pl.BlockSpec((1, 1, 1, D_MODEL), lambda b, s: (b, jnp.minimum(s, 1), 0, 0))],
        out_specs=pl.BlockSpec((1, ROW_BLOCK, D_MODEL), lambda b, s: (b, s, 0)),
        out_shape=jax.ShapeDtypeStruct((BATCH, SEQ_ALL, D_MODEL), F32),
        compiler_params=_cparams("parallel", "parallel"), name="out_proj_residual",
    )(a, w, x, gate)


def _diff_attn_kernel(lam_ref, q_ref, k_ref, v_ref, gain_ref, o_ref, *, out_scale):
    lam = lam_ref[0]
    lane = lax.broadcasted_iota(jnp.int32, (1, LANES), 1)
    in_map0 = lane < DIFF_HEAD_DIM

    def attend(n_keys):
        q = q_ref[0]
        k = k_ref[0, :n_keys, :]
        v = v_ref[0, :n_keys, :]
        zero = jnp.zeros_like(q)
        nt = (((1,), (1,)), ((), ()))
        s0 = lax.dot_general(jnp.where(in_map0, q, zero), k, nt, preferred_element_type=F32)
        s1 = lax.dot_general(jnp.where(in_map0, zero, q), k, nt, preferred_element_type=F32)
        p0 = jnp.exp(s0 - jnp.max(s0, axis=-1, keepdims=True))
        p1 = jnp.exp(s1 - jnp.max(s1, axis=-1, keepdims=True))
        r0 = 1.0 / jnp.sum(p0, axis=-1, keepdims=True)
        r1 = lam / jnp.sum(p1, axis=-1, keepdims=True)
        a = p0 * r0 - p1 * r1
        o = jnp.dot(a.astype(BF16), v, preferred_element_type=F32)
        o = o * lax.rsqrt(jnp.mean(o * o, axis=-1, keepdims=True) + NORM_EPS)
        o_ref[0] = (o * gain_ref[...] * out_scale).astype(o_ref.dtype)

    is_ctx = pl.program_id(2) == 0

    @pl.when(is_ctx)
    def _():
        attend(CTX_LEN)

    @pl.when(jnp.logical_not(is_ctx))
    def _():
        attend(SEQ_ALL)


def diff_attention(qkv, lam, gain, out_scale):
    kv_blocks = DIFF_QK_WIDTH // LANES
    return pl.pallas_call(
        functools.partial(_diff_attn_kernel, out_scale=out_scale),
        grid=(BATCH, DIFF_HEADS, N_ROW_BLOCKS),
        in_specs=[pl.BlockSpec(memory_space=pltpu.SMEM),
                  pl.BlockSpec((1, ROW_BLOCK, LANES), lambda b, h, s: (b, s, h)),
                  pl.BlockSpec((1, SEQ_ALL, LANES), lambda b, h, s: (b, 0, kv_blocks + h)),
                  pl.BlockSpec((1, SEQ_ALL, LANES), lambda b, h, s: (b, 0, 2 * kv_blocks + h)),
                  pl.BlockSpec((1, LANES), lambda b, h, s: (0, 0))],
        out_specs=pl.BlockSpec((1, ROW_BLOCK, LANES), lambda b, h, s: (b, s, h)),
        out_shape=jax.ShapeDtypeStruct((BATCH, SEQ_ALL, DIFF_V_WIDTH), BF16),
        compiler_params=_cparams("parallel", "parallel", "arbitrary"), name="diff_attention",
    )(lam.reshape(1), qkv, qkv, qkv, gain.reshape(1, LANES))


def rope_tables():
    t = jnp.arange(SEQ)
    pos = jnp.stack([t // GRID_W, t % GRID_W], axis=-1).astype(F32)
    inv = ROPE_BASE ** (-jnp.arange(ROPE_FREQS, dtype=F32) / ROPE_FREQS)
    ang = pos[:, :, None] * inv
    cos = jnp.cos(ang)
    sin = jnp.sin(ang)
    cos_l = jnp.tile(jnp.concatenate([cos, cos], axis=-1).reshape(SEQ, 4 * ROPE_FREQS), (1, 2))
    sin_l = jnp.tile(jnp.concatenate([-sin, sin], axis=-1).reshape(SEQ, 4 * ROPE_FREQS), (1, 2))
    cos_t = jnp.concatenate([jnp.ones((CTX_LEN, LANES), F32), cos_l], axis=0)
    sin_t = jnp.concatenate([jnp.zeros((CTX_LEN, LANES), F32), sin_l], axis=0)
    return cos_t, sin_t


def _rmsnorm(x, g):
    xf = x.astype(F32)
    y = xf * lax.rsqrt(jnp.mean(xf * xf, axis=-1, keepdims=True) + NORM_EPS)
    return (y * g.astype(F32)).astype(x.dtype)


def _l2norm(x):
    x = x.astype(F32)
    return x * lax.rsqrt(jnp.sum(x * x, axis=-1, keepdims=True) + NORM_EPS)


def _depthwise_conv(x, w):
    return lax.conv_general_dilated(x, w[:, None, :].astype(x.dtype), window_strides=(1,),
                                    padding=[(SHORT_CONV // 2, SHORT_CONV // 2)],
                                    dimension_numbers=('NWC', 'WIO', 'NWC'),
                                    feature_group_count=x.shape[-1])


def _gated_delta_chunked(q, k, v, beta, g, s0):
    B, H, L, dk = q.shape
    dv = v.shape[-1]
    C = GDN_CHUNK
    n = L // C
    hp = lax.Precision.HIGHEST
    q, k, v, beta, g = [t.astype(F32).reshape(B, H, n, C, *t.shape[3:]) for t in (q, k, v, beta, g)]
    G = jnp.cumsum(g, axis=-1)
    causal = jnp.tril(jnp.ones((C, C), bool))
    strict = jnp.tril(jnp.ones((C, C), bool), -1)
    gdiff = G[..., :, None] - G[..., None, :]
    decay = jnp.where(causal, jnp.exp(jnp.where(causal, gdiff, 0.0)), 0.0)
    kk = jnp.einsum('bhncd,bhnjd->bhncj', k, k, precision=hp)
    a_mat = jnp.where(strict, beta[..., :, None] * kk * decay, 0.0)
    eye = jnp.eye(C, dtype=F32)
    rhs = jnp.concatenate([beta[..., None] * v, (beta * jnp.exp(G))[..., None] * k], axis=-1)
    sol = lax.linalg.triangular_solve(eye + a_mat, rhs, left_side=True, lower=True, unit_diagonal=True)
    w_intra, k_cum = sol[..., :dv], sol[..., dv:]
    qk = jnp.einsum('bhncd,bhnjd->bhncj', q, k, precision=hp) * decay
    q_dec = q * jnp.exp(G)[..., None]
    k_dec = k * jnp.exp(G[..., -1:] - G)[..., None]
    chunk_decay = jnp.exp(G[..., -1])

    def step(s, inp):
        w_i, kc_i, qk_i, qd_i, kd_i, cd_i = inp
        w = w_i - jnp.einsum('bhcd,bhde->bhce', kc_i, s, precision=hp)
        o = (jnp.einsum('bhcd,bhde->bhce', qd_i, s, precision=hp)
             + jnp.einsum('bhcj,bhje->bhce', qk_i, w, precision=hp))
        s = s * cd_i[..., None, None] + jnp.einsum('bhcd,bhce->bhde', kd_i, w, precision=hp)
        return s, o

    xs = tuple(jnp.moveaxis(t, 2, 0) for t in (w_intra, k_cum, qk, q_dec, k_dec, chunk_decay))
    s_fin, o = lax.scan(step, s0.astype(F32), xs)
    o = jnp.moveaxis(o, 0, 2).reshape(B, H, L, dv)
    return o, s_fin


def _gdn_inputs(qkv, ba, a_log, dt_bias):
    B, L, _ = qkv.shape
    qkv = jnp.transpose(jax.nn.silu(qkv).reshape(B, L, 3, GDN_HEADS, GDN_HEAD_DIM), (2, 0, 3, 1, 4))
    q = _l2norm(qkv[0]) * (GDN_HEAD_DIM ** -0.5)
    k = _l2norm(qkv[1])
    v = qkv[2]
    ba = ba.astype(F32).reshape(B, L, 2, 2, GDN_HEADS)
    beta = jax.nn.sigmoid(ba[:, :, 0])
    g = -jnp.exp(a_log.astype(F32)) * jax.nn.softplus(ba[:, :, 1] + dt_bias.astype(F32))
    return q, k, v, jnp.transpose(beta, (2, 0, 3, 1)), jnp.transpose(g, (2, 0, 3, 1))


def _gdn_bidir(q, k, v, beta, g, s0):
    o_f, s_f = _gated_delta_chunked(q, k, v, beta[0], g[0], s0[0])
    fl = lambda t: jnp.flip(t, axis=2)
    o_b, s_b = _gated_delta_chunked(fl(q), fl(k), fl(v), fl(beta[1]), fl(g[1]), s0[1])
    return o_f + fl(o_b), jnp.stack([s_f, s_b])


def _hyena_filters(L, w1, b1, w2, b2, w3, b3, w4, freq):
    hp = lax.Precision.HIGHEST
    t = jnp.linspace(0.0, 1.0, L, dtype=F32)
    bands = (HYENA_EMB - 1) // 2
    wpos = 2.0 * math.pi * jnp.arange(L, dtype=F32) / L
    fb = jnp.linspace(1e-4, bands - 1, bands, dtype=F32)
    z = jnp.concatenate([t[:, None], jnp.cos(wpos[:, None] * fb), -jnp.sin(wpos[:, None] * fb)], axis=-1)
    h = jnp.sin(freq * (jnp.dot(z, w1, precision=hp) + b1))
    h = jnp.sin(freq * (jnp.dot(h, w2, precision=hp) + b2))
    h = jnp.sin(freq * (jnp.dot(h, w3, precision=hp) + b3))
    h = jnp.dot(h, w4, precision=hp).astype(F32).reshape(L, HYENA_ORDER, 2, HYENA_WIDTH)
    deltas = jnp.linspace(HYENA_MIN_DECAY, HYENA_MAX_DECAY, HYENA_WIDTH, dtype=F32)
    window = jnp.exp(-t[:, None] * jnp.abs(deltas))
    return h * window[:, None, None, :]


def _two_sided_long_conv(u, h_fwd, h_bwd, bias):
    B, L, C = u.shape
    filt2 = jnp.concatenate([h_fwd, jnp.zeros((1, C), F32), jnp.flip(h_bwd[1:], axis=0)], axis=0)
    uf = jnp.fft.rfft(u.astype(F32), n=2 * L, axis=1)
    ff = jnp.fft.rfft(filt2, n=2 * L, axis=0)
    y = jnp.fft.irfft(uf * ff[None], n=2 * L, axis=1)[:, :L]
    return (y + u.astype(F32) * bias.astype(F32)).astype(u.dtype)


def _hyena(xh, filt, bias):
    z = xh[:, :, 2]
    for o in range(HYENA_ORDER):
        z = xh[:, :, o] * _two_sided_long_conv(z, filt[:, o, 0], filt[:, o, 1], bias[o])
    return z


def _even_mixer_core(p, conv_w, a_log, dt_bias, gdn_gain, hf, hy_bias):
    def split(pp):
        B, L, _ = pp.shape
        cv = _depthwise_conv(pp[..., :N_CONV], conv_w)
        return (cv[..., :3 * GDN_WIDTH], cv[..., 3 * GDN_WIDTH:].reshape(B, L, 3, HYENA_WIDTH),
                pp[..., N_CONV:N_CONV + GDN_WIDTH], pp[..., N_CONV + GDN_WIDTH:])

    def gdn_out(o, z):
        B, H, L, dv = o.shape
        o = jnp.transpose(o, (0, 2, 1, 3)).astype(z.dtype)
        o = _rmsnorm(o, gdn_gain) * jax.nn.silu(z.reshape(B, L, H, dv))
        return o.reshape(B, L, H * dv)

    def hyena_branch(xh):
        return _hyena(xh, _hyena_filters(xh.shape[1], *hf), hy_bias)

    B = p.shape[0]
    qkv_c, hy_c, z_c, ba_c = split(p[:, :CTX_LEN])
    q, k, v, beta, g = _gdn_inputs(qkv_c, ba_c, a_log, dt_bias)
    s0 = jnp.zeros((2, B, GDN_HEADS, GDN_HEAD_DIM, GDN_HEAD_DIM), F32)
    o_c, s_c = _gdn_bidir(q, k, v, beta, g, s0)
    qkv_l, hy_l, z_l, ba_l = split(p[:, CTX_LEN:])
    q, k, v, beta, g = _gdn_inputs(qkv_l, ba_l, a_log, dt_bias)
    o_l, _ = _gdn_bidir(q, k, v, beta, g, s_c)
    a_lat = jnp.concatenate([gdn_out(o_l, z_l), hyena_branch(hy_l)], axis=-1)
    a_ctx = jnp.concatenate([gdn_out(o_c, z_c), hyena_branch(hy_c)], axis=-1)
    return jnp.concatenate([a_ctx, a_lat], axis=1)


SEL_GROUPS = 2 * PEER_HEADS
SEL_CANDS = PEER_TOPK * PEER_TOPK
_HIGHEST = lax.Precision.HIGHEST
_NT = (((1,), (1,)), ((), ()))


def _peer_select_kernel(q_ref, keys_ref, idx_ref, gate_ref, s_scr, sv_scr, si_scr, cand_scr, cidx_scr, val_scr):
    n_tok = PEER_TOKENS_PER_STEP
    row_f = lax.broadcasted_iota(jnp.int32, (PEER_KEYS, n_tok), 0).astype(F32)
    row2_f = lax.broadcasted_iota(jnp.int32, (SEL_CANDS, n_tok), 0).astype(F32)
    neg_inf = jnp.float32(-jnp.inf)

    for g in range(SEL_GROUPS):
        qg = q_ref[:, g * LANES:(g + 1) * LANES]
        s_scr[g] = lax.dot_general(keys_ref[g], qg, _NT, precision=_HIGHEST, preferred_element_type=F32)

    def key_pick(k, c):
        for g in range(SEL_GROUPS):
            s = s_scr[g]
            m = jnp.max(s, axis=0, keepdims=True)
            pos = jnp.min(jnp.where(s == m, row_f, float(PEER_KEYS)), axis=0, keepdims=True)
            s_scr[g] = jnp.where(row_f == pos, neg_inf, s)
            sv_scr[g, pl.ds(k, 1), :] = m
            si_scr[g, pl.ds(k, 1), :] = pos
        return c

    lax.fori_loop(0, PEER_TOPK, key_pick, 0)

    for h in range(PEER_HEADS):
        sv0, sv1 = sv_scr[2 * h], sv_scr[2 * h + 1]
        si0, si1 = si_scr[2 * h], si_scr[2 * h + 1]
        for a in range(PEER_TOPK):
            cand_scr[h, a * PEER_TOPK:(a + 1) * PEER_TOPK, :] = sv0[a:a + 1, :] + sv1
            cidx_scr[h, a * PEER_TOPK:(a + 1) * PEER_TOPK, :] = si0[a:a + 1, :] * float(PEER_KEYS) + si1

    def expert_pick(k, c):
        for h in range(PEER_HEADS):
            cd = cand_scr[h]
            m = jnp.max(cd, axis=0, keepdims=True)
            pos = jnp.min(jnp.where(cd == m, row2_f, float(2 * SEL_CANDS)), axis=0, keepdims=True)
            hit = row2_f == pos
            e = jnp.max(jnp.where(hit, cidx_scr[h], -1.0), axis=0, keepdims=True)
            cand_scr[h] = jnp.where(hit, neg_inf, cd)
            val_scr[pl.ds(h * PEER_TOPK + k, 1), :] = m
            idx_ref[0, pl.ds(h * PEER_TOPK + k, 1), :] = e.astype(jnp.int32)
        return c

    lax.fori_loop(0, PEER_TOPK, expert_pick, 0)

    for h in range(PEER_HEADS):
        v = val_scr[h * PEER_TOPK:(h + 1) * PEER_TOPK, :]
        ex = jnp.exp(v - v[0:1, :])
        gate_ref[0, h * PEER_TOPK:(h + 1) * PEER_TOPK, :] = ex / jnp.sum(ex, axis=0, keepdims=True)


def peer_select(q, keys):
    n_steps = q.shape[0] // PEER_TOKENS_PER_STEP
    out_spec = pl.BlockSpec((1, PEER_PICKS, PEER_TOKENS_PER_STEP), lambda i: (i, 0, 0))
    scr = lambda n, r: pltpu.VMEM((n, r, PEER_TOKENS_PER_STEP), F32)
    return pl.pallas_call(
        _peer_select_kernel, grid=(n_steps,),
        in_specs=[pl.BlockSpec((PEER_TOKENS_PER_STEP, PEER_HEADS * PEER_QDIM), lambda i: (i, 0)),
                  pl.BlockSpec((SEL_GROUPS, PEER_KEYS, PEER_QDIM // 2), lambda i: (0, 0, 0))],
        out_specs=[out_spec, out_spec],
        out_shape=[jax.ShapeDtypeStruct((n_steps, PEER_PICKS, PEER_TOKENS_PER_STEP), jnp.int32),
                   jax.ShapeDtypeStruct((n_steps, PEER_PICKS, PEER_TOKENS_PER_STEP), F32)],
        scratch_shapes=[scr(SEL_GROUPS, PEER_KEYS), scr(SEL_GROUPS, PEER_TOPK), scr(SEL_GROUPS, PEER_TOPK),
                        scr(PEER_HEADS, SEL_CANDS), scr(PEER_HEADS, SEL_CANDS),
                        pltpu.VMEM((PEER_PICKS, PEER_TOKENS_PER_STEP), F32)],
        compiler_params=_cparams("parallel"), name="peer_select",
    )(q, keys.reshape(SEL_GROUPS, PEER_KEYS, PEER_QDIM // 2))


def pack_expert_table(tab):
    bits = lax.bitcast_convert_type(tab.astype(BF16), jnp.uint16).astype(jnp.uint32)
    bits = bits.reshape(N_EXPERTS // 2, 2, D_MODEL // LANES, LANES)
    return (bits[:, 0] | (bits[:, 1] << 16)).reshape(N_EXPERTS // 2 * ROW_TILE, LANES)


def _expert_row(tab_ref, offset, shift):
    w = tab_ref[pl.ds(pl.multiple_of(offset, ROW_TILE), ROW_TILE), :]
    return lax.bitcast_convert_type((w << shift.astype(jnp.uint32)) & jnp.uint32(0xFFFF0000), F32)


def _peer_act_kernel(off_ref, shift_ref, h_ref, tab_ref, act_ref):
    lane = lax.broadcasted_iota(jnp.int32, (ROW_TILE, LANES), 1)

    def token(t, carry):
        h = h_ref[t]
        acc = jnp.zeros((ROW_TILE, LANES), F32)
        for p in range(PEER_PICKS):
            s = p * PEER_TOKENS_PER_STEP + t
            row = _expert_row(tab_ref, off_ref[s], shift_ref[s])
            acc = jnp.where(lane == p, jnp.sum(row * h, axis=1, keepdims=True), acc)
        act_ref[pl.ds(t, 1), :] = jnp.sum(acc, axis=0, keepdims=True)
        return carry

    lax.fori_loop(0, PEER_TOKENS_PER_STEP, token, 0)


def _peer_out_kernel(off_ref, shift_ref, coef_ref, x_ref, gate_ref, tab_ref, o_ref):
    n_acc = 4
    group = 16

    def token(t, carry):
        def picks(g, accs):
            accs = list(accs)
            base = g * (group * PEER_TOKENS_PER_STEP) + t
            for j in range(group):
                s = base + j * PEER_TOKENS_PER_STEP
                row = _expert_row(tab_ref, off_ref[s], shift_ref[s])
                accs[j % n_acc] = accs[j % n_acc] + coef_ref[s] * row
            return tuple(accs)

        zero = jnp.zeros((ROW_TILE, LANES), F32)
        accs = lax.fori_loop(0, PEER_PICKS // group, picks, (zero,) * n_acc)
        f = (accs[0] + accs[1]) + (accs[2] + accs[3])
        o_ref[t] = x_ref[t] + gate_ref[0, 0] * f
        return carry

    lax.fori_loop(0, PEER_TOKENS_PER_STEP, token, 0)


def _peer_specs():
    n_tok = BATCH * SEQ_ALL
    steps_per_batch = SEQ_ALL // PEER_TOKENS_PER_STEP
    ctx_steps = CTX_LEN // PEER_TOKENS_PER_STEP
    flat = PEER_TOKENS_PER_STEP * PEER_PICKS
    smem = pl.BlockSpec((flat,), lambda i: (i,), memory_space=pltpu.SMEM)
    tok = pl.BlockSpec((PEER_TOKENS_PER_STEP, ROW_TILE, LANES), lambda i: (i, 0, 0))
    tab = pl.BlockSpec((N_EXPERTS // 2 * ROW_TILE, LANES), lambda i: (0, 0), pipeline_mode=pl.Buffered(1))
    gate = pl.BlockSpec((1, 1, ROW_TILE, LANES),
                        lambda i: (i // steps_per_batch, jnp.minimum((i % steps_per_batch) // ctx_steps, 1), 0, 0))
    picks = pl.BlockSpec((PEER_TOKENS_PER_STEP, PEER_PICKS), lambda i: (i, 0))
    return n_tok, smem, tok, tab, gate, picks


def peer_activations(offset, shift, h, tab):
    n_tok, smem, tok, tabspec, _, picks = _peer_specs()
    return pl.pallas_call(
        _peer_act_kernel, grid=(n_tok // PEER_TOKENS_PER_STEP,),
        in_specs=[smem, smem, tok, tabspec], out_specs=picks,
        out_shape=jax.ShapeDtypeStruct((n_tok, PEER_PICKS), F32),
        compiler_params=_cparams("parallel"), name="peer_activations",
    )(offset, shift, h.reshape(n_tok, ROW_TILE, LANES), tab)


def peer_output_residual(offset, shift, coef, x, gate, tab):
    n_tok, smem, tok, tabspec, gatespec, _ = _peer_specs()
    out = pl.pallas_call(
        _peer_out_kernel, grid=(n_tok // PEER_TOKENS_PER_STEP,),
        in_specs=[smem, smem, smem, tok, gatespec, tabspec], out_specs=tok,
        out_shape=jax.ShapeDtypeStruct((n_tok, ROW_TILE, LANES), F32),
        compiler_params=_cparams("parallel"), name="peer_output_residual",
    )(offset, shift, coef, x.reshape(n_tok, ROW_TILE, LANES), gate.reshape(BATCH, 2, ROW_TILE, LANES), tab)
    return out.reshape(BATCH, SEQ_ALL, D_MODEL)


def kernel(x, c, ctx, c_ctx, norm1, norm2, w_ada, b_ada, w_in, conv_w, a_log, dt_bias, gdn_gain, hf_w1, hf_b1, hf_w2, hf_b2, hf_w3, hf_b3, hf_w4, hf_freq, hy_bias, w_out_even, w_qkv, lam_q1, lam_k1, lam_q2, lam_k2, subln, w_out_odd, peer_wq, peer_keys, peer_u, peer_v, final_norm):
    hp = lax.Precision.HIGHEST
    xs = jnp.concatenate([ctx, x], axis=1)
    cos_tab, sin_tab = rope_tables()
    cond = jnp.concatenate([jax.nn.silu(c_ctx)[None], jax.nn.silu(c)], axis=0)
    for layer in range(DEPTH):
        m = (jnp.dot(cond, w_ada[layer], precision=hp) + b_ada[layer]).reshape(BATCH + 1, 6, D_MODEL)
        m = jnp.stack([jnp.broadcast_to(m[0], (BATCH, 6, D_MODEL)), m[1:]], axis=1)
        mod1, gate1 = m[:, :, 0:2], m[:, :, 2:3]
        mod2, gate2 = m[:, :, 3:5], m[:, :, 5:6]
        if layer % 2 == 0:
            e = layer // 2
            w_pad = jnp.pad(w_in[e], ((0, 0), (0, N_IN_PAD - N_IN))).astype(BF16)
            p = norm_mod_proj(xs, norm1[layer], mod1, w_pad)[..., :N_IN]
            hf = (hf_w1[e], hf_b1[e], hf_w2[e], hf_b2[e], hf_w3[e], hf_b3[e], hf_w4[e], hf_freq[e])
            a = _even_mixer_core(p, conv_w[e], a_log[e], dt_bias[e], gdn_gain[e], hf, hy_bias[e])
            xs = out_proj_residual(a, w_out_even[e].astype(BF16), xs, gate1)
        else:
            o = layer // 2
            lam_init = 0.8 - 0.6 * math.exp(-0.3 * layer)
            lam = (jnp.exp(jnp.sum(lam_q1[o].astype(F32) * lam_k1[o].astype(F32)))
                   - jnp.exp(jnp.sum(lam_q2[o].astype(F32) * lam_k2[o].astype(F32))) + lam_init)
            qkv = norm_mod_proj_rope(xs, norm1[layer], mod1, w_qkv[o].astype(BF16), cos_tab, sin_tab)
            a = diff_attention(qkv, lam, subln[o], 1.0 - lam_init)
            xs = out_proj_residual(a, w_out_odd[o].astype(BF16), xs, gate1)
        h2, q = norm_mod_proj_h(xs, norm2[layer], mod2, peer_wq[layer].astype(BF16))
        idx, gate = peer_select(q.reshape(BATCH * SEQ_ALL, -1), peer_keys[layer])
        offset = ((idx >> 1) * ROW_TILE).reshape(-1)
        shift = (16 - 16 * (idx & 1)).reshape(-1)
        act = peer_activations(offset, shift, h2, pack_expert_table(peer_u[layer]))
        act = jnp.swapaxes(act.reshape(-1, PEER_TOKENS_PER_STEP, PEER_PICKS), 1, 2)
        coef = (gate * jax.nn.gelu(act, approximate=False)).reshape(-1)
        xs = peer_output_residual(offset, shift, coef, xs, gate2, pack_expert_table(peer_v[layer]))
    return _rmsnorm(xs[:, CTX_LEN:], final_norm)
```

```python
import functools
import math

import jax
import jax.numpy as jnp
from jax import lax
from jax.experimental import pallas as pl
from jax.experimental.pallas import tpu as pltpu

F32 = jnp.float32
BF16 = jnp.bfloat16

D_MODEL = 1024
BATCH = 8
SEQ = 4096
DEPTH = 4
GRID_W = 64
CTX_LEN = 256
SEQ_ALL = CTX_LEN + SEQ
ROW_BLOCK = CTX_LEN
N_ROW_BLOCKS = SEQ_ALL // ROW_BLOCK

GDN_HEAD_DIM = 128
GDN_WIDTH = D_MODEL // 2
GDN_HEADS = GDN_WIDTH // GDN_HEAD_DIM
GDN_CHUNK = 64
SHORT_CONV = 3

HYENA_WIDTH = D_MODEL // 2
HYENA_ORDER = 2
HYENA_EMB = 33
HYENA_DECAY_TARGET = 1e-2
HYENA_MIN_DECAY = math.log(HYENA_DECAY_TARGET) / 1.5
HYENA_MAX_DECAY = math.log(HYENA_DECAY_TARGET) / 0.3

N_CONV = 3 * GDN_WIDTH + 3 * HYENA_WIDTH
N_IN = N_CONV + GDN_WIDTH + 4 * GDN_HEADS
LANES = 128
N_IN_PAD = -(-N_IN // LANES) * LANES

DIFF_HEAD_DIM = 64
DIFF_HEADS = D_MODEL // (2 * DIFF_HEAD_DIM)
DIFF_V_DIM = 2 * DIFF_HEAD_DIM
DIFF_QK_WIDTH = DIFF_HEADS * 2 * DIFF_HEAD_DIM
DIFF_V_WIDTH = DIFF_HEADS * DIFF_V_DIM
ROPE_BASE = 10000.0
ROPE_FREQS = DIFF_HEAD_DIM // 4

PEER_HEADS = 8
PEER_KEYS = 128
PEER_TOPK = 16
PEER_QDIM = 256
PEER_TOKEN_BLOCK = 128
PEER_PICKS = PEER_HEADS * PEER_TOPK
N_EXPERTS = PEER_KEYS * PEER_KEYS
PEER_TOKENS_PER_STEP = 128
ROW_TILE = 8
NORM_EPS = 1e-6

VMEM_LIMIT = 56 * 1024 * 1024


def _cparams(*sem):
    return pltpu.CompilerParams(dimension_semantics=sem, vmem_limit_bytes=VMEM_LIMIT)


def _norm_mod(x, g, mod):
    y = x * lax.rsqrt(jnp.mean(x * x, axis=-1, keepdims=True) + NORM_EPS) * g
    return y * (1.0 + mod[1:2, :]) + mod[0:1, :]


def _proj_kernel(x_ref, g_ref, mod_ref, w_ref, o_ref):
    h = _norm_mod(x_ref[0], g_ref[...], mod_ref[0, 0])
    o_ref[0] = jnp.dot(h.astype(BF16), w_ref[...], preferred_element_type=F32).astype(o_ref.dtype)


def _proj_h_kernel(x_ref, g_ref, mod_ref, w_ref, h_ref, o_ref):
    h = _norm_mod(x_ref[0], g_ref[...], mod_ref[0, 0])
    h_ref[0] = h
    o_ref[0] = jnp.dot(h.astype(BF16), w_ref[...], preferred_element_type=F32)


def _proj_rope_kernel(x_ref, g_ref, mod_ref, w_ref, cos_ref, sin_ref, o_ref):
    h = _norm_mod(x_ref[0], g_ref[...], mod_ref[0, 0])
    p = jnp.dot(h.astype(BF16), w_ref[...], preferred_element_type=F32)
    cos = cos_ref[...]
    sin = sin_ref[...]
    lane = lax.broadcasted_iota(jnp.int32, (1, LANES), 1)
    first_half = (lane % (2 * ROPE_FREQS)) < ROPE_FREQS
    n_rot = 2 * DIFF_QK_WIDTH // LANES
    for j in range(p.shape[1] // LANES):
        blk = p[:, j * LANES:(j + 1) * LANES]
        if j < n_rot:
            partner = jnp.where(first_half, pltpu.roll(blk, LANES - ROPE_FREQS, 1),
                                pltpu.roll(blk, ROPE_FREQS, 1))
            blk = blk * cos + partner * sin
            if j < n_rot // 2:
                blk = blk * (DIFF_HEAD_DIM ** -0.5)
        o_ref[0, :, j * LANES:(j + 1) * LANES] = blk.astype(o_ref.dtype)


def _row_specs(n_out):
    x_spec = pl.BlockSpec((1, ROW_BLOCK, D_MODEL), lambda b, s: (b, s, 0))
    g_spec = pl.BlockSpec((1, D_MODEL), lambda b, s: (0, 0))
    mod_spec = pl.BlockSpec((1, 1, 2, D_MODEL), lambda b, s: (b, jnp.minimum(s, 1), 0, 0))
    w_spec = pl.BlockSpec((D_MODEL, n_out), lambda b, s: (0, 0))
    o_spec = pl.BlockSpec((1, ROW_BLOCK, n_out), lambda b, s: (b, s, 0))
    return x_spec, g_spec, mod_spec, w_spec, o_spec


def norm_mod_proj(x, g, mod, w, out_dtype=F32):
    n_out = w.shape[1]
    x_spec, g_spec, mod_spec, w_spec, o_spec = _row_specs(n_out)
    return pl.pallas_call(
        _proj_kernel, grid=(BATCH, N_ROW_BLOCKS),
        in_specs=[x_spec, g_spec, mod_spec, w_spec], out_specs=o_spec,
        out_shape=jax.ShapeDtypeStruct((BATCH, SEQ_ALL, n_out), out_dtype),
        compiler_params=_cparams("parallel", "parallel"), name="norm_mod_proj",
    )(x, g.reshape(1, D_MODEL), mod, w)


def norm_mod_proj_h(x, g, mod, w):
    n_out = w.shape[1]
    x_spec, g_spec, mod_spec, w_spec, o_spec = _row_specs(n_out)
    return pl.pallas_call(
        _proj_h_kernel, grid=(BATCH, N_ROW_BLOCKS),
        in_specs=[x_spec, g_spec, mod_spec, w_spec], out_specs=[x_spec, o_spec],
        out_shape=[jax.ShapeDtypeStruct((BATCH, SEQ_ALL, D_MODEL), F32),
                   jax.ShapeDtypeStruct((BATCH, SEQ_ALL, n_out), F32)],
        compiler_params=_cparams("parallel", "parallel"), name="norm_mod_proj_h",
    )(x, g.reshape(1, D_MODEL), mod, w)


def norm_mod_proj_rope(x, g, mod, w, cos_tab, sin_tab):
    n_out = w.shape[1]
    x_spec, g_spec, mod_spec, w_spec, o_spec = _row_specs(n_out)
    tab_spec = pl.BlockSpec((ROW_BLOCK, LANES), lambda b, s: (s, 0))
    return pl.pallas_call(
        _proj_rope_kernel, grid=(BATCH, N_ROW_BLOCKS),
        in_specs=[x_spec, g_spec, mod_spec, w_spec, tab_spec, tab_spec], out_specs=o_spec,
        out_shape=jax.ShapeDtypeStruct((BATCH, SEQ_ALL, n_out), BF16),
        compiler_params=_cparams("parallel", "parallel"), name="norm_mod_proj_rope",
    )(x, g.reshape(1, D_MODEL), mod, w, cos_tab, sin_tab)


def _out_proj_kernel(a_ref, w_ref, x_ref, gate_ref, o_ref):
    y = jnp.dot(a_ref[0].astype(BF16), w_ref[...], preferred_element_type=F32)
    o_ref[0] = x_ref[0] + gate_ref[0, 0] * y


def out_proj_residual(a, w, x, gate):
    k = a.shape[-1]
    return pl.pallas_call(
        _out_proj_kernel, grid=(BATCH, N_ROW_BLOCKS),
        in_specs=[pl.BlockSpec((1, ROW_BLOCK, k), lambda b, s: (b, s, 0)),
                  pl.BlockSpec((k, D_MODEL), lambda b, s: (0, 0)),
                  pl.BlockSpec((1, ROW_BLOCK, D_MODEL), lambda b, s: (b, s, 0)),
                  pl.BlockSpec((1, 1, 1, D_MODEL), lambda b, s: (b, jnp.minimum(s, 1), 0, 0))],
        out_specs=pl.BlockSpec((1, ROW_BLOCK, D_MODEL), lambda b, s: (b, s, 0)),
        out_shape=jax.ShapeDtypeStruct((BATCH, SEQ_ALL, D_MODEL), F32),
        compiler_params=_cparams("parallel", "parallel"), name="out_proj_residual",
    )(a, w, x, gate)


def _diff_attn_kernel(lam_ref, q_ref, k_ref, v_ref, gain_ref, o_ref, *, out_scale):
    lam = lam_ref[0]
    lane = lax.broadcasted_iota(jnp.int32, (1, LANES), 1)
    in_map0 = lane < DIFF_HEAD_DIM

    def attend(n_keys):
        q = q_ref[0]
        k = k_ref[0, :n_keys, :]
        v = v_ref[0, :n_keys, :]
        zero = jnp.zeros_like(q)
        nt = (((1,), (1,)), ((), ()))
        s0 = lax.dot_general(jnp.where(in_map0, q, zero), k, nt, preferred_element_type=F32)
        s1 = lax.dot_general(jnp.where(in_map0, zero, q), k, nt, preferred_element_type=F32)
        p0 = jnp.exp(s0 - jnp.max(s0, axis=-1, keepdims=True))
        p1 = jnp.exp(s1 - jnp.max(s1, axis=-1, keepdims=True))
        r0 = 1.0 / jnp.sum(p0, axis=-1, keepdims=True)
        r1 = lam / jnp.sum(p1, axis=-1, keepdims=True)
        a = p0 * r0 - p1 * r1
        o = jnp.dot(a.astype(BF16), v, preferred_element_type=F32)
        o = o * lax.rsqrt(jnp.mean(o * o, axis=-1, keepdims=True) + NORM_EPS)
        o_ref[0] = (o * gain_ref[...] * out_scale).astype(o_ref.dtype)

    is_ctx = pl.program_id(2) == 0

    @pl.when(is_ctx)
    def _():
        attend(CTX_LEN)

    @pl.when(jnp.logical_not(is_ctx))
    def _():
        attend(SEQ_ALL)


def diff_attention(qkv, lam, gain, out_scale):
    kv_blocks = DIFF_QK_WIDTH // LANES
    return pl.pallas_call(
        functools.partial(_diff_attn_kernel, out_scale=out_scale),
        grid=(BATCH, DIFF_HEADS, N_ROW_BLOCKS),
        in_specs=[pl.BlockSpec(memory_space=pltpu.SMEM),
                  pl.BlockSpec((1, ROW_BLOCK, LANES), lambda b, h, s: (b, s, h)),
                  pl.BlockSpec((1, SEQ_ALL, LANES), lambda b, h, s: (b, 0, kv_blocks + h)),
                  pl.BlockSpec((1, SEQ_ALL, LANES), lambda b, h, s: (b, 0, 2 * kv_blocks + h)),
                  pl.BlockSpec((1, LANES), lambda b, h, s: (0, 0))],
        out_specs=pl.BlockSpec((1, ROW_BLOCK, LANES), lambda b, h, s: (b, s, h)),
        out_shape=jax.ShapeDtypeStruct((BATCH, SEQ_ALL, DIFF_V_WIDTH), BF16),
        compiler_params=_cparams("parallel", "parallel", "arbitrary"), name="diff_attention",
    )(lam.reshape(1), qkv, qkv, qkv, gain.reshape(1, LANES))


def rope_tables():
    t = jnp.arange(SEQ)
    pos = jnp.stack([t // GRID_W, t % GRID_W], axis=-1).astype(F32)
    inv = ROPE_BASE ** (-jnp.arange(ROPE_FREQS, dtype=F32) / ROPE_FREQS)
    ang = pos[:, :, None] * inv
    cos = jnp.cos(ang)
    sin = jnp.sin(ang)
    cos_l = jnp.tile(jnp.concatenate([cos, cos], axis=-1).reshape(SEQ, 4 * ROPE_FREQS), (1, 2))
    sin_l = jnp.tile(jnp.concatenate([-sin, sin], axis=-1).reshape(SEQ, 4 * ROPE_FREQS), (1, 2))
    cos_t = jnp.concatenate([jnp.ones((CTX_LEN, LANES), F32), cos_l], axis=0)
    sin_t = jnp.concatenate([jnp.zeros((CTX_LEN, LANES), F32), sin_l], axis=0)
    return cos_t, sin_t


def _rmsnorm(x, g):
    xf = x.astype(F32)
    y = xf * lax.rsqrt(jnp.mean(xf * xf, axis=-1, keepdims=True) + NORM_EPS)
    return (y * g.astype(F32)).astype(x.dtype)


def _l2norm(x):
    x = x.astype(F32)
    return x * lax.rsqrt(jnp.sum(x * x, axis=-1, keepdims=True) + NORM_EPS)


def _depthwise_conv(x, w):
    return lax.conv_general_dilated(x, w[:, None, :].astype(x.dtype), window_strides=(1,),
                                    padding=[(SHORT_CONV // 2, SHORT_CONV // 2)],
                                    dimension_numbers=('NWC', 'WIO', 'NWC'),
                                    feature_group_count=x.shape[-1])


def _gated_delta_chunked(q, k, v, beta, g, s0):
    B, H, L, dk = q.shape
    dv = v.shape[-1]
    C = GDN_CHUNK
    n = L // C
    hp = lax.Precision.HIGHEST
    q, k, v, beta, g = [t.astype(F32).reshape(B, H, n, C, *t.shape[3:]) for t in (q, k, v, beta, g)]
    G = jnp.cumsum(g, axis=-1)
    causal = jnp.tril(jnp.ones((C, C), bool))
    strict = jnp.tril(jnp.ones((C, C), bool), -1)
    gdiff = G[..., :, None] - G[..., None, :]
    decay = jnp.where(causal, jnp.exp(jnp.where(causal, gdiff, 0.0)), 0.0)
    kk = jnp.einsum('bhncd,bhnjd->bhncj', k, k, precision=hp)
    a_mat = jnp.where(strict, beta[..., :, None] * kk * decay, 0.0)
    eye = jnp.eye(C, dtype=F32)
    rhs = jnp.concatenate([beta[..., None] * v, (beta * jnp.exp(G))[..., None] * k], axis=-1)
    sol = lax.linalg.triangular_solve(eye + a_mat, rhs, left_side=True, lower=True, unit_diagonal=True)
    w_intra, k_cum = sol[..., :dv], sol[..., dv:]
    qk = jnp.einsum('bhncd,bhnjd->bhncj', q, k, precision=hp) * decay
    q_dec = q * jnp.exp(G)[..., None]
    k_dec = k * jnp.exp(G[..., -1:] - G)[..., None]
    chunk_decay = jnp.exp(G[..., -1])

    def step(s, inp):
        w_i, kc_i, qk_i, qd_i, kd_i, cd_i = inp
        w = w_i - jnp.einsum('bhcd,bhde->bhce', kc_i, s, precision=hp)
        o = (jnp.einsum('bhcd,bhde->bhce', qd_i, s, precision=hp)
             + jnp.einsum('bhcj,bhje->bhce', qk_i, w, precision=hp))
        s = s * cd_i[..., None, None] + jnp.einsum('bhcd,bhce->bhde', kd_i, w, precision=hp)
        return s, o

    xs = tuple(jnp.moveaxis(t, 2, 0) for t in (w_intra, k_cum, qk, q_dec, k_dec, chunk_decay))
    s_fin, o = lax.scan(step, s0.astype(F32), xs)
    o = jnp.moveaxis(o, 0, 2).reshape(B, H, L, dv)
    return o, s_fin


def _gdn_inputs(qkv, ba, a_log, dt_bias):
    B, L, _ = qkv.shape
    qkv = jnp.transpose(jax.nn.silu(qkv).reshape(B, L, 3, GDN_HEADS, GDN_HEAD_DIM), (2, 0, 3, 1, 4))
    q = _l2norm(qkv[0]) * (GDN_HEAD_DIM ** -0.5)
    k = _l2norm(qkv[1])
    v = qkv[2]
    ba = ba.astype(F32).reshape(B, L, 2, 2, GDN_HEADS)
    beta = jax.nn.sigmoid(ba[:, :, 0])
    g = -jnp.exp(a_log.astype(F32)) * jax.nn.softplus(ba[:, :, 1] + dt_bias.astype(F32))
    return q, k, v, jnp.transpose(beta, (2, 0, 3, 1)), jnp.transpose(g, (2, 0, 3, 1))


def _gdn_bidir(q, k, v, beta, g, s0):
    o_f, s_f = _gated_delta_chunked(q, k, v, beta[0], g[0], s0[0])
    fl = lambda t: jnp.flip(t, axis=2)
    anti_eye = jnp.flip(jnp.eye(GDN_CHUNK, dtype=F32), axis=0)

    def fl_rows(t):
        B, H, L, D = t.shape
        t = jnp.flip(t.reshape(B, H, L // GDN_CHUNK, GDN_CHUNK, D), axis=2)
        return jnp.einsum('ij,bhnjd->bhnid', anti_eye, t, precision=lax.Precision.HIGHEST).reshape(B, H, L, D)

    o_b, s_b = _gated_delta_chunked(fl_rows(q), fl_rows(k), fl_rows(v), fl(beta[1]), fl(g[1]), s0[1])
    return o_f + fl_rows(o_b), jnp.stack([s_f, s_b])


def _hyena_filters(L, w1, b1, w2, b2, w3, b3, w4, freq):
    hp = lax.Precision.HIGHEST
    t = jnp.linspace(0.0, 1.0, L, dtype=F32)
    bands = (HYENA_EMB - 1) // 2
    wpos = 2.0 * math.pi * jnp.arange(L, dtype=F32) / L
    fb = jnp.linspace(1e-4, bands - 1, bands, dtype=F32)
    z = jnp.concatenate([t[:, None], jnp.cos(wpos[:, None] * fb), -jnp.sin(wpos[:, None] * fb)], axis=-1)
    h = jnp.sin(freq * (jnp.dot(z, w1, precision=hp) + b1))
    h = jnp.sin(freq * (jnp.dot(h, w2, precision=hp) + b2))
    h = jnp.sin(freq * (jnp.dot(h, w3, precision=hp) + b3))
    h = jnp.dot(h, w4, precision=hp).astype(F32).reshape(L, HYENA_ORDER, 2, HYENA_WIDTH)
    deltas = jnp.linspace(HYENA_MIN_DECAY, HYENA_MAX_DECAY, HYENA_WIDTH, dtype=F32)
    window = jnp.exp(-t[:, None] * jnp.abs(deltas))
    return h * window[:, None, None, :]


def _two_sided_long_conv(u, h_fwd, h_bwd, bias):
    B, L, C = u.shape
    filt2 = jnp.concatenate([h_fwd, jnp.zeros((1, C), F32), jnp.flip(h_bwd[1:], axis=0)], axis=0)
    uf = jnp.fft.rfft(u.astype(F32), n=2 * L, axis=1)
    ff = jnp.fft.rfft(filt2, n=2 * L, axis=0)
    y = jnp.fft.irfft(uf * ff[None], n=2 * L, axis=1)[:, :L]
    return (y + u.astype(F32) * bias.astype(F32)).astype(u.dtype)


def _hyena(xh, filt, bias):
    z = xh[:, :, 2]
    for o in range(HYENA_ORDER):
        z = xh[:, :, o] * _two_sided_long_conv(z, filt[:, o, 0], filt[:, o, 1], bias[o])
    return z


def _even_mixer_core(p, conv_w, a_log, dt_bias, gdn_gain, hf, hy_bias):
    def split(pp):
        B, L, _ = pp.shape
        cv = _depthwise_conv(pp[..., :N_CONV], conv_w)
        return (cv[..., :3 * GDN_WIDTH], cv[..., 3 * GDN_WIDTH:].reshape(B, L, 3, HYENA_WIDTH),
                pp[..., N_CONV:N_CONV + GDN_WIDTH], pp[..., N_CONV + GDN_WIDTH:])

    def gdn_out(o, z):
        B, H, L, dv = o.shape
        o = jnp.transpose(o, (0, 2, 1, 3)).astype(z.dtype)
        o = _rmsnorm(o, gdn_gain) * jax.nn.silu(z.reshape(B, L, H, dv))
        return o.reshape(B, L, H * dv)

    def hyena_branch(xh):
        return _hyena(xh, _hyena_filters(xh.shape[1], *hf), hy_bias)

    B = p.shape[0]
    qkv_c, hy_c, z_c, ba_c = split(p[:, :CTX_LEN])
    q, k, v, beta, g = _gdn_inputs(qkv_c, ba_c, a_log, dt_bias)
    s0 = jnp.zeros((2, B, GDN_HEADS, GDN_HEAD_DIM, GDN_HEAD_DIM), F32)
    o_c, s_c = _gdn_bidir(q, k, v, beta, g, s0)
    qkv_l, hy_l, z_l, ba_l = split(p[:, CTX_LEN:])
    q, k, v, beta, g = _gdn_inputs(qkv_l, ba_l, a_log, dt_bias)
    o_l, _ = _gdn_bidir(q, k, v, beta, g, s_c)
    a_lat = jnp.concatenate([gdn_out(o_l, z_l), hyena_branch(hy_l)], axis=-1)
    a_ctx = jnp.concatenate([gdn_out(o_c, z_c), hyena_branch(hy_c)], axis=-1)
    return jnp.concatenate([a_ctx, a_lat], axis=1)


SEL_GROUPS = 2 * PEER_HEADS
SEL_CANDS = -(-sum(PEER_TOPK // (a + 1) for a in range(PEER_TOPK)) // ROW_TILE) * ROW_TILE
_HIGHEST = lax.Precision.HIGHEST
_NT = (((1,), (1,)), ((), ()))


def _peer_select_kernel(q_ref, keys_ref, idx_ref, gate_ref, s_scr, sv_scr, si_scr, cand_scr, cidx_scr, val_scr):
    n_tok = PEER_TOKENS_PER_STEP
    row_f = lax.broadcasted_iota(jnp.int32, (PEER_KEYS, n_tok), 0).astype(F32)
    row2_f = lax.broadcasted_iota(jnp.int32, (SEL_CANDS, n_tok), 0).astype(F32)
    neg_inf = jnp.float32(-jnp.inf)

    for g in range(SEL_GROUPS):
        qg = q_ref[:, g * LANES:(g + 1) * LANES]
        s_scr[g] = lax.dot_general(keys_ref[g], qg, _NT, precision=_HIGHEST, preferred_element_type=F32)

    def key_pick(k, c):
        for g in range(SEL_GROUPS):
            s = s_scr[g]
            m = jnp.max(s, axis=0, keepdims=True)
            pos = jnp.min(jnp.where(s == m, row_f, float(PEER_KEYS)), axis=0, keepdims=True)
            s_scr[g] = jnp.where(row_f == pos, neg_inf, s)
            sv_scr[g, pl.ds(k, 1), :] = m
            si_scr[g, pl.ds(k, 1), :] = pos
        return c

    lax.fori_loop(0, PEER_TOPK, key_pick, 0)

    for h in range(PEER_HEADS):
        sv0, sv1 = sv_scr[2 * h], sv_scr[2 * h + 1]
        si0, si1 = si_scr[2 * h], si_scr[2 * h + 1]
        cand_scr[h] = jnp.full((SEL_CANDS, n_tok), neg_inf, F32)
        cidx_scr[h] = jnp.zeros((SEL_CANDS, n_tok), F32)
        r0 = 0
        for a in range(PEER_TOPK):
            nb = PEER_TOPK // (a + 1)
            cand_scr[h, r0:r0 + nb, :] = sv0[a:a + 1, :] + sv1[0:nb, :]
            cidx_scr[h, r0:r0 + nb, :] = si0[a:a + 1, :] * float(PEER_KEYS) + si1[0:nb, :]
            r0 += nb

    def expert_pick(k, c):
        for h in range(PEER_HEADS):
            cd = cand_scr[h]
            m = jnp.max(cd, axis=0, keepdims=True)
            pos = jnp.min(jnp.where(cd == m, row2_f, float(2 * SEL_CANDS)), axis=0, keepdims=True)
            hit = row2_f == pos
            e = jnp.max(jnp.where(hit, cidx_scr[h], -1.0), axis=0, keepdims=True)
            cand_scr[h] = jnp.where(hit, neg_inf, cd)
            val_scr[pl.ds(h * PEER_TOPK + k, 1), :] = m
            idx_ref[0, pl.ds(h * PEER_TOPK + k, 1), :] = e.astype(jnp.int32)
        return c

    lax.fori_loop(0, PEER_TOPK, expert_pick, 0)

    for h in range(PEER_HEADS):
        v = val_scr[h * PEER_TOPK:(h + 1) * PEER_TOPK, :]
        ex = jnp.exp(v - v[0:1, :])
        gate_ref[0, h * PEER_TOPK:(h + 1) * PEER_TOPK, :] = ex / jnp.sum(ex, axis=0, keepdims=True)


def peer_select(q, keys):
    n_steps = q.shape[0] // PEER_TOKENS_PER_STEP
    out_spec = pl.BlockSpec((1, PEER_PICKS, PEER_TOKENS_PER_STEP), lambda i: (i, 0, 0))
    scr = lambda n, r: pltpu.VMEM((n, r, PEER_TOKENS_PER_STEP), F32)
    return pl.pallas_call(
        _peer_select_kernel, grid=(n_steps,),
        in_specs=[pl.BlockSpec((PEER_TOKENS_PER_STEP, PEER_HEADS * PEER_QDIM), lambda i: (i, 0)),
                  pl.BlockSpec((SEL_GROUPS, PEER_KEYS, PEER_QDIM // 2), lambda i: (0, 0, 0))],
        out_specs=[out_spec, out_spec],
        out_shape=[jax.ShapeDtypeStruct((n_steps, PEER_PICKS, PEER_TOKENS_PER_STEP), jnp.int32),
                   jax.ShapeDtypeStruct((n_steps, PEER_PICKS, PEER_TOKENS_PER_STEP), F32)],
        scratch_shapes=[scr(SEL_GROUPS, PEER_KEYS), scr(SEL_GROUPS, PEER_TOPK), scr(SEL_GROUPS, PEER_TOPK),
                        scr(PEER_HEADS, SEL_CANDS), scr(PEER_HEADS, SEL_CANDS),
                        pltpu.VMEM((PEER_PICKS, PEER_TOKENS_PER_STEP), F32)],
        compiler_params=_cparams("parallel"), name="peer_select",
    )(q, keys.reshape(SEL_GROUPS, PEER_KEYS, PEER_QDIM // 2))


def pack_expert_table(tab):
    bits = lax.bitcast_convert_type(tab.astype(BF16), jnp.uint16).astype(jnp.uint32)
    bits = bits.reshape(N_EXPERTS // 2, 2, D_MODEL // LANES, LANES)
    return (bits[:, 0] | (bits[:, 1] << 16)).reshape(N_EXPERTS // 2 * ROW_TILE, LANES)


def _expert_row(tab_ref, offset, shift):
    w = tab_ref[pl.ds(pl.multiple_of(offset, ROW_TILE), ROW_TILE), :]
    return lax.bitcast_convert_type((w << shift.astype(jnp.uint32)) & jnp.uint32(0xFFFF0000), F32)


def _peer_act_kernel(off_ref, shift_ref, h_ref, tab_ref, act_ref):
    lane = lax.broadcasted_iota(jnp.int32, (ROW_TILE, LANES), 1)

    def token(t, carry):
        h = h_ref[t]
        acc = jnp.zeros((ROW_TILE, LANES), F32)
        for p in range(PEER_PICKS):
            s = p * PEER_TOKENS_PER_STEP + t
            row = _expert_row(tab_ref, off_ref[s], shift_ref[s])
            acc = jnp.where(lane == p, jnp.sum(row * h, axis=1, keepdims=True), acc)
        act_ref[pl.ds(t, 1), :] = jnp.sum(acc, axis=0, keepdims=True)
        return carry

    lax.fori_loop(0, PEER_TOKENS_PER_STEP, token, 0)


def _peer_out_kernel(off_ref, shift_ref, coef_ref, x_ref, gate_ref, tab_ref, o_ref):
    n_acc = 4
    group = 16

    def token(t, carry):
        def picks(g, accs):
            accs = list(accs)
            base = g * (group * PEER_TOKENS_PER_STEP) + t
            for j in range(group):
                s = base + j * PEER_TOKENS_PER_STEP
                row = _expert_row(tab_ref, off_ref[s], shift_ref[s])
                accs[j % n_acc] = accs[j % n_acc] + coef_ref[s] * row
            return tuple(accs)

        zero = jnp.zeros((ROW_TILE, LANES), F32)
        accs = lax.fori_loop(0, PEER_PICKS // group, picks, (zero,) * n_acc)
        f = (accs[0] + accs[1]) + (accs[2] + accs[3])
        o_ref[t] = x_ref[t] + gate_ref[0, 0] * f
        return carry

    lax.fori_loop(0, PEER_TOKENS_PER_STEP, token, 0)


def _peer_specs():
    n_tok = BATCH * SEQ_ALL
    steps_per_batch = SEQ_ALL // PEER_TOKENS_PER_STEP
    ctx_steps = CTX_LEN // PEER_TOKENS_PER_STEP
    flat = PEER_TOKENS_PER_STEP * PEER_PICKS
    smem = pl.BlockSpec((flat,), lambda i: (i,), memory_space=pltpu.SMEM)
    tok = pl.BlockSpec((PEER_TOKENS_PER_STEP, ROW_TILE, LANES), lambda i: (i, 0, 0))
    tab = pl.BlockSpec((N_EXPERTS // 2 * ROW_TILE, LANES), lambda i: (0, 0), pipeline_mode=pl.Buffered(1))
    gate = pl.BlockSpec((1, 1, ROW_TILE, LANES),
                        lambda i: (i // steps_per_batch, jnp.minimum((i % steps_per_batch) // ctx_steps, 1), 0, 0))
    picks = pl.BlockSpec((PEER_TOKENS_PER_STEP, PEER_PICKS), lambda i: (i, 0))
    return n_tok, smem, tok, tab, gate, picks


def peer_activations(offset, shift, h, tab):
    n_tok, smem, tok, tabspec, _, picks = _peer_specs()
    return pl.pallas_call(
        _peer_act_kernel, grid=(n_tok // PEER_TOKENS_PER_STEP,),
        in_specs=[smem, smem, tok, tabspec], out_specs=picks,
        out_shape=jax.ShapeDtypeStruct((n_tok, PEER_PICKS), F32),
        compiler_params=_cparams("parallel"), name="peer_activations",
    )(offset, shift, h.reshape(n_tok, ROW_TILE, LANES), tab)


def peer_output_residual(offset, shift, coef, x, gate, tab):
    n_tok, smem, tok, tabspec, gatespec, _ = _peer_specs()
    out = pl.pallas_call(
        _peer_out_kernel, grid=(n_tok // PEER_TOKENS_PER_STEP,),
        in_specs=[smem, smem, smem, tok, gatespec, tabspec], out_specs=tok,
        out_shape=jax.ShapeDtypeStruct((n_tok, ROW_TILE, LANES), F32),
        compiler_params=_cparams("parallel"), name="peer_output_residual",
    )(offset, shift, coef, x.reshape(n_tok, ROW_TILE, LANES), gate.reshape(BATCH, 2, ROW_TILE, LANES), tab)
    return out.reshape(BATCH, SEQ_ALL, D_MODEL)


def kernel(x, c, ctx, c_ctx, norm1, norm2, w_ada, b_ada, w_in, conv_w, a_log, dt_bias, gdn_gain, hf_w1, hf_b1, hf_w2, hf_b2, hf_w3, hf_b3, hf_w4, hf_freq, hy_bias, w_out_even, w_qkv, lam_q1, lam_k1, lam_q2, lam_k2, subln, w_out_odd, peer_wq, peer_keys, peer_u, peer_v, final_norm):
    hp = lax.Precision.HIGHEST
    xs = jnp.concatenate([ctx, x], axis=1)
    cos_tab, sin_tab = rope_tables()
    cond = jnp.concatenate([jax.nn.silu(c_ctx)[None], jax.nn.silu(c)], axis=0)
    for layer in range(DEPTH):
        m = (jnp.dot(cond, w_ada[layer], precision=hp) + b_ada[layer]).reshape(BATCH + 1, 6, D_MODEL)
        m = jnp.stack([jnp.broadcast_to(m[0], (BATCH, 6, D_MODEL)), m[1:]], axis=1)
        mod1, gate1 = m[:, :, 0:2], m[:, :, 2:3]
        mod2, gate2 = m[:, :, 3:5], m[:, :, 5:6]
        if layer % 2 == 0:
            e = layer // 2
            w_pad = jnp.pad(w_in[e], ((0, 0), (0, N_IN_PAD - N_IN))).astype(BF16)
            p = norm_mod_proj(xs, norm1[layer], mod1, w_pad)[..., :N_IN]
            hf = (hf_w1[e], hf_b1[e], hf_w2[e], hf_b2[e], hf_w3[e], hf_b3[e], hf_w4[e], hf_freq[e])
            a = _even_mixer_core(p, conv_w[e], a_log[e], dt_bias[e], gdn_gain[e], hf, hy_bias[e])
            xs = out_proj_residual(a, w_out_even[e].astype(BF16), xs, gate1)
        else:
            o = layer // 2
            lam_init = 0.8 - 0.6 * math.exp(-0.3 * layer)
            lam = (jnp.exp(jnp.sum(lam_q1[o].astype(F32) * lam_k1[o].astype(F32)))
                   - jnp.exp(jnp.sum(lam_q2[o].astype(F32) * lam_k2[o].astype(F32))) + lam_init)
            qkv = norm_mod_proj_rope(xs, norm1[layer], mod1, w_qkv[o].astype(BF16), cos_tab, sin_tab)
            a = diff_attention(qkv, lam, subln[o], 1.0 - lam_init)
            xs = out_proj_residual(a, w_out_odd[o].astype(BF16), xs, gate1)
        h2, q = norm_mod_proj_h(xs, norm2[layer], mod2, peer_wq[layer].astype(BF16))
        idx, gate = peer_select(q.reshape(BATCH * SEQ_ALL, -1), peer_keys[layer])
        offset = ((idx >> 1) * ROW_TILE).reshape(-1)
        shift = (16 - 16 * (idx & 1)).reshape(-1)
        act = peer_activations(offset, shift, h2, pack_expert_table(peer_u[layer]))
        act = jnp.swapaxes(act.reshape(-1, PEER_TOKENS_PER_STEP, PEER_PICKS), 1, 2)
        coef = (gate * jax.nn.gelu(act, approximate=False)).reshape(-1)
        xs = peer_output_residual(offset, shift, coef, xs, gate2, pack_expert_table(peer_v[layer]))
    return _rmsnorm(xs[:, CTX_LEN:], final_norm)
```

```python
import functools
import math

import jax
import jax.numpy as jnp
from jax import lax
from jax.experimental import pallas as pl
from jax.experimental.pallas import tpu as pltpu

F32 = jnp.float32
BF16 = jnp.bfloat16

D_MODEL = 1024
BATCH = 8
SEQ = 4096
DEPTH = 4
GRID_W = 64
CTX_LEN = 256
SEQ_ALL = CTX_LEN + SEQ
ROW_BLOCK = CTX_LEN
N_ROW_BLOCKS = SEQ_ALL // ROW_BLOCK

GDN_HEAD_DIM = 128
GDN_WIDTH = D_MODEL // 2
GDN_HEADS = GDN_WIDTH // GDN_HEAD_DIM
GDN_CHUNK = 64
SHORT_CONV = 3

HYENA_WIDTH = D_MODEL // 2
HYENA_ORDER = 2
HYENA_EMB = 33
HYENA_DECAY_TARGET = 1e-2
HYENA_MIN_DECAY = math.log(HYENA_DECAY_TARGET) / 1.5
HYENA_MAX_DECAY = math.log(HYENA_DECAY_TARGET) / 0.3

N_CONV = 3 * GDN_WIDTH + 3 * HYENA_WIDTH
N_IN = N_CONV + GDN_WIDTH + 4 * GDN_HEADS
LANES = 128
N_IN_PAD = -(-N_IN // LANES) * LANES

DIFF_HEAD_DIM = 64
DIFF_HEADS = D_MODEL // (2 * DIFF_HEAD_DIM)
DIFF_V_DIM = 2 * DIFF_HEAD_DIM
DIFF_QK_WIDTH = DIFF_HEADS * 2 * DIFF_HEAD_DIM
DIFF_V_WIDTH = DIFF_HEADS * DIFF_V_DIM
ROPE_BASE = 10000.0
ROPE_FREQS = DIFF_HEAD_DIM // 4

PEER_HEADS = 8
PEER_KEYS = 128
PEER_TOPK = 16
PEER_QDIM = 256
PEER_TOKEN_BLOCK = 128
PEER_PICKS = PEER_HEADS * PEER_TOPK
N_EXPERTS = PEER_KEYS * PEER_KEYS
PEER_TOKENS_PER_STEP = 128
ROW_TILE = 8
NORM_EPS = 1e-6

VMEM_LIMIT = 56 * 1024 * 1024


def _cparams(*sem):
    return pltpu.CompilerParams(dimension_semantics=sem, vmem_limit_bytes=VMEM_LIMIT)


def _norm_mod(x, g, mod):
    y = x * lax.rsqrt(jnp.mean(x * x, axis=-1, keepdims=True) + NORM_EPS) * g
    return y * (1.0 + mod[1:2, :]) + mod[0:1, :]


def _proj_kernel(x_ref, g_ref, mod_ref, w_ref, o_ref):
    h = _norm_mod(x_ref[0], g_ref[...], mod_ref[0, 0])
    o_ref[0] = jnp.dot(h.astype(BF16), w_ref[...], preferred_element_type=F32).astype(o_ref.dtype)


def _proj_h_kernel(x_ref, g_ref, mod_ref, w_ref, h_ref, o_ref):
    h = _norm_mod(x_ref[0], g_ref[...], mod_ref[0, 0])
    h_ref[0] = h
    o_ref[0] = jnp.dot(h.astype(BF16), w_ref[...], preferred_element_type=F32)


def _proj_rope_kernel(x_ref, g_ref, mod_ref, w_ref, cos_ref, sin_ref, o_ref):
    h = _norm_mod(x_ref[0], g_ref[...], mod_ref[0, 0])
    p = jnp.dot(h.astype(BF16), w_ref[...], preferred_element_type=F32)
    cos = cos_ref[...]
    sin = sin_ref[...]
    lane = lax.broadcasted_iota(jnp.int32, (1, LANES), 1)
    first_half = (lane % (2 * ROPE_FREQS)) < ROPE_FREQS
    n_rot = 2 * DIFF_QK_WIDTH // LANES
    for j in range(p.shape[1] // LANES):
        blk = p[:, j * LANES:(j + 1) * LANES]
        if j < n_rot:
            partner = jnp.where(first_half, pltpu.roll(blk, LANES - ROPE_FREQS, 1),
                                pltpu.roll(blk, ROPE_FREQS, 1))
            blk = blk * cos + partner * sin
            if j < n_rot // 2:
                blk = blk * (DIFF_HEAD_DIM ** -0.5)
        o_ref[0, :, j * LANES:(j + 1) * LANES] = blk.astype(o_ref.dtype)


def _row_specs(n_out):
    x_spec = pl.BlockSpec((1, ROW_BLOCK, D_MODEL), lambda b, s: (b, s, 0))
    g_spec = pl.BlockSpec((1, D_MODEL), lambda b, s: (0, 0))
    mod_spec = pl.BlockSpec((1, 1, 2, D_MODEL), lambda b, s: (b, jnp.minimum(s, 1), 0, 0))
    w_spec = pl.BlockSpec((D_MODEL, n_out), lambda b, s: (0, 0))
    o_spec = pl.BlockSpec((1, ROW_BLOCK, n_out), lambda b, s: (b, s, 0))
    return x_spec, g_spec, mod_spec, w_spec, o_spec


def norm_mod_proj(x, g, mod, w, out_dtype=F32):
    n_out = w.shape[1]
    x_spec, g_spec, mod_spec, w_spec, o_spec = _row_specs(n_out)
    return pl.pallas_call(
        _proj_kernel, grid=(BATCH, N_ROW_BLOCKS),
        in_specs=[x_spec, g_spec, mod_spec, w_spec], out_specs=o_spec,
        out_shape=jax.ShapeDtypeStruct((BATCH, SEQ_ALL, n_out), out_dtype),
        compiler_params=_cparams("parallel", "parallel"), name="norm_mod_proj",
    )(x, g.reshape(1, D_MODEL), mod, w)


def norm_mod_proj_h(x, g, mod, w):
    n_out = w.shape[1]
    x_spec, g_spec, mod_spec, w_spec, o_spec = _row_specs(n_out)
    return pl.pallas_call(
        _proj_h_kernel, grid=(BATCH, N_ROW_BLOCKS),
        in_specs=[x_spec, g_spec, mod_spec, w_spec], out_specs=[x_spec, o_spec],
        out_shape=[jax.ShapeDtypeStruct((BATCH, SEQ_ALL, D_MODEL), F32),
                   jax.ShapeDtypeStruct((BATCH, SEQ_ALL, n_out), F32)],
        compiler_params=_cparams("parallel", "parallel"), name="norm_mod_proj_h",
    )(x, g.reshape(1, D_MODEL), mod, w)


def norm_mod_proj_rope(x, g, mod, w, cos_tab, sin_tab):
    n_out = w.shape[1]
    x_spec, g_spec, mod_spec, w_spec, o_spec = _row_specs(n_out)
    tab_spec = pl.BlockSpec((ROW_BLOCK, LANES), lambda b, s: (s, 0))
    return pl.pallas_call(
        _proj_rope_kernel, grid=(BATCH, N_ROW_BLOCKS),
        in_specs=[x_spec, g_spec, mod_spec, w_spec, tab_spec, tab_spec], out_specs=o_spec,
        out_shape=jax.ShapeDtypeStruct((BATCH, SEQ_ALL, n_out), BF16),
        compiler_params=_cparams("parallel", "parallel"), name="norm_mod_proj_rope",
    )(x, g.reshape(1, D_MODEL), mod, w, cos_tab, sin_tab)


def _out_proj_kernel(a_ref, w_ref, x_ref, gate_ref, o_ref):
    y = jnp.dot(a_ref[0].astype(BF16), w_ref[...], preferred_element_type=F32)
    o_ref[0] = x_ref[0] + gate_ref[0, 0] * y


def out_proj_residual(a, w, x, gate):
    k = a.shape[-1]
    return pl.pallas_call(
        _out_proj_kernel, grid=(BATCH, N_ROW_BLOCKS),
        in_specs=[pl.BlockSpec((1, ROW_BLOCK, k), lambda b, s: (b, s, 0)),
                  pl.BlockSpec((k, D_MODEL), lambda b, s: (0, 0)),
                  pl.BlockSpec((1, ROW_BLOCK, D_MODEL), lambda b, s: (b, s, 0)),
                  pl.BlockSpec((1, 1, 1, D_MODEL), lambda b, s: (b, jnp.minimum(s, 1), 0, 0))],
        out_specs=pl.BlockSpec((1, ROW_BLOCK, D_MODEL), lambda b, s: (b, s, 0)),
        out_shape=jax.ShapeDtypeStruct((BATCH, SEQ_ALL, D_MODEL), F32),
        compiler_params=_cparams("parallel", "parallel"), name="out_proj_residual",
    )(a, w, x, gate)


def _diff_attn_kernel(lam_ref, q_ref, k_ref, v_ref, gain_ref, o_ref, *, out_scale):
    lam = lam_ref[0]
    lane = lax.broadcasted_iota(jnp.int32, (1, LANES), 1)
    in_map0 = lane < DIFF_HEAD_DIM

    def attend(n_keys):
        q = q_ref[0]
        k = k_ref[0, :n_keys, :]
        v = v_ref[0, :n_keys, :]
        zero = jnp.zeros_like(q)
        nt = (((1,), (1,)), ((), ()))
        s0 = lax.dot_general(jnp.where(in_map0, q, zero), k, nt, preferred_element_type=F32)
        s1 = lax.dot_general(jnp.where(in_map0, zero, q), k, nt, preferred_element_type=F32)
        p0 = jnp.exp(s0 - jnp.max(s0, axis=-1, keepdims=True))
        p1 = jnp.exp(s1 - jnp.max(s1, axis=-1, keepdims=True))
        r0 = 1.0 / jnp.sum(p0, axis=-1, keepdims=True)
        r1 = lam / jnp.sum(p1, axis=-1, keepdims=True)
        a = p0 * r0 - p1 * r1
        o = jnp.dot(a.astype(BF16), v, preferred_element_type=F32)
        o = o * lax.rsqrt(jnp.mean(o * o, axis=-1, keepdims=True) + NORM_EPS)
        o_ref[0] = (o * gain_ref[...] * out_scale).astype(o_ref.dtype)

    is_ctx = pl.program_id(2) == 0

    @pl.when(is_ctx)
    def _():
        attend(CTX_LEN)

    @pl.when(jnp.logical_not(is_ctx))
    def _():
        attend(SEQ_ALL)


def diff_attention(qkv, lam, gain, out_scale):
    kv_blocks = DIFF_QK_WIDTH // LANES
    return pl.pallas_call(
        functools.partial(_diff_attn_kernel, out_scale=out_scale),
        grid=(BATCH, DIFF_HEADS, N_ROW_BLOCKS),
        in_specs=[pl.BlockSpec(memory_space=pltpu.SMEM),
                  pl.BlockSpec((1, ROW_BLOCK, LANES), lambda b, h, s: (b, s, h)),
                  pl.BlockSpec((1, SEQ_ALL, LANES), lambda b, h, s: (b, 0, kv_blocks + h)),
                  pl.BlockSpec((1, SEQ_ALL, LANES), lambda b, h, s: (b, 0, 2 * kv_blocks + h)),
                  pl.BlockSpec((1, LANES), lambda b, h, s: (0, 0))],
        out_specs=pl.BlockSpec((1, ROW_BLOCK, LANES), lambda b, h, s: (b, s, h)),
        out_shape=jax.ShapeDtypeStruct((BATCH, SEQ_ALL, DIFF_V_WIDTH), BF16),
        compiler_params=_cparams("parallel", "parallel", "arbitrary"), name="diff_attention",
    )(lam.reshape(1), qkv, qkv, qkv, gain.reshape(1, LANES))


def rope_tables():
    t = jnp.arange(SEQ)
    pos = jnp.stack([t // GRID_W, t % GRID_W], axis=-1).astype(F32)
    inv = ROPE_BASE ** (-jnp.arange(ROPE_FREQS, dtype=F32) / ROPE_FREQS)
    ang = pos[:, :, None] * inv
    cos = jnp.cos(ang)
    sin = jnp.sin(ang)
    cos_l = jnp.tile(jnp.concatenate([cos, cos], axis=-1).reshape(SEQ, 4 * ROPE_FREQS), (1, 2))
    sin_l = jnp.tile(jnp.concatenate([-sin, sin], axis=-1).reshape(SEQ, 4 * ROPE_FREQS), (1, 2))
    cos_t = jnp.concatenate([jnp.ones((CTX_LEN, LANES), F32), cos_l], axis=0)
    sin_t = jnp.concatenate([jnp.zeros((CTX_LEN, LANES), F32), sin_l], axis=0)
    return cos_t, sin_t


def _rmsnorm(x, g):
    xf = x.astype(F32)
    y = xf * lax.rsqrt(jnp.mean(xf * xf, axis=-1, keepdims=True) + NORM_EPS)
    return (y * g.astype(F32)).astype(x.dtype)


def _l2norm(x):
    x = x.astype(F32)
    return x * lax.rsqrt(jnp.sum(x * x, axis=-1, keepdims=True) + NORM_EPS)


def _depthwise_conv(x, w):
    return lax.conv_general_dilated(x, w[:, None, :].astype(x.dtype), window_strides=(1,),
                                    padding=[(SHORT_CONV // 2, SHORT_CONV // 2)],
                                    dimension_numbers=('NWC', 'WIO', 'NWC'),
                                    feature_group_count=x.shape[-1])


def _gated_delta_chunked(q, k, v, beta, g, s0, reverse=False):
    B, H, L, dk = q.shape
    dv = v.shape[-1]
    C = GDN_CHUNK
    n = L // C
    hp = lax.Precision.HIGHEST
    q, k, v, beta, g = [t.astype(F32).reshape(B, H, n, C, *t.shape[3:]) for t in (q, k, v, beta, g)]
    G = jnp.cumsum(g, axis=-1)
    causal = jnp.tril(jnp.ones((C, C), bool))
    strict = jnp.tril(jnp.ones((C, C), bool), -1)
    gdiff = G[..., :, None] - G[..., None, :]
    decay = jnp.where(causal, jnp.exp(jnp.where(causal, gdiff, 0.0)), 0.0)
    kk = jnp.einsum('bhncd,bhnjd->bhncj', k, k, precision=hp)
    a_mat = jnp.where(strict, beta[..., :, None] * kk * decay, 0.0)
    eye = jnp.eye(C, dtype=F32)
    rhs = jnp.concatenate([beta[..., None] * v, (beta * jnp.exp(G))[..., None] * k], axis=-1)
    sol = lax.linalg.triangular_solve(eye + a_mat, rhs, left_side=True, lower=True, unit_diagonal=True)
    w_intra, k_cum = sol[..., :dv], sol[..., dv:]
    qk = jnp.einsum('bhncd,bhnjd->bhncj', q, k, precision=hp) * decay
    q_dec = q * jnp.exp(G)[..., None]
    k_dec = k * jnp.exp(G[..., -1:] - G)[..., None]
    chunk_decay = jnp.exp(G[..., -1])

    def step(s, inp):
        w_i, kc_i, qk_i, qd_i, kd_i, cd_i = inp
        w = w_i - jnp.einsum('bhcd,bhde->bhce', kc_i, s, precision=hp)
        o = (jnp.einsum('bhcd,bhde->bhce', qd_i, s, precision=hp)
             + jnp.einsum('bhcj,bhje->bhce', qk_i, w, precision=hp))
        s = s * cd_i[..., None, None] + jnp.einsum('bhcd,bhce->bhde', kd_i, w, precision=hp)
        return s, o

    xs = tuple(jnp.moveaxis(t, 2, 0) for t in (w_intra, k_cum, qk, q_dec, k_dec, chunk_decay))
    s_fin, o = lax.scan(step, s0.astype(F32), xs, reverse=reverse)
    o = jnp.moveaxis(o, 0, 2).reshape(B, H, L, dv)
    return o, s_fin


def _gdn_inputs(qkv, ba, a_log, dt_bias):
    B, L, _ = qkv.shape
    qkv = jnp.transpose(jax.nn.silu(qkv).reshape(B, L, 3, GDN_HEADS, GDN_HEAD_DIM), (2, 0, 3, 1, 4))
    q = _l2norm(qkv[0]) * (GDN_HEAD_DIM ** -0.5)
    k = _l2norm(qkv[1])
    v = qkv[2]
    ba = ba.astype(F32).reshape(B, L, 2, 2, GDN_HEADS)
    beta = jax.nn.sigmoid(ba[:, :, 0])
    g = -jnp.exp(a_log.astype(F32)) * jax.nn.softplus(ba[:, :, 1] + dt_bias.astype(F32))
    return q, k, v, jnp.transpose(beta, (2, 0, 3, 1)), jnp.transpose(g, (2, 0, 3, 1))


def _gdn_bidir(q, k, v, beta, g, s0):
    o_f, s_f = _gated_delta_chunked(q, k, v, beta[0], g[0], s0[0])
    anti_eye = jnp.flip(jnp.eye(GDN_CHUNK, dtype=F32), axis=0)

    def fl_rows(t):
        B, H, L, D = t.shape
        t = t.reshape(B, H, L // GDN_CHUNK, GDN_CHUNK, D)
        return jnp.einsum('ij,bhnjd->bhnid', anti_eye, t, precision=lax.Precision.HIGHEST).reshape(B, H, L, D)

    def fl_scalars(t):
        B, H, L = t.shape
        return jnp.flip(t.reshape(B, H, L // GDN_CHUNK, GDN_CHUNK), axis=3).reshape(B, H, L)

    o_b, s_b = _gated_delta_chunked(fl_rows(q), fl_rows(k), fl_rows(v), fl_scalars(beta[1]), fl_scalars(g[1]),
                                    s0[1], reverse=True)
    return o_f + fl_rows(o_b), jnp.stack([s_f, s_b])


def _hyena_filters(L, w1, b1, w2, b2, w3, b3, w4, freq):
    hp = lax.Precision.HIGHEST
    t = jnp.linspace(0.0, 1.0, L, dtype=F32)
    bands = (HYENA_EMB - 1) // 2
    wpos = 2.0 * math.pi * jnp.arange(L, dtype=F32) / L
    fb = jnp.linspace(1e-4, bands - 1, bands, dtype=F32)
    z = jnp.concatenate([t[:, None], jnp.cos(wpos[:, None] * fb), -jnp.sin(wpos[:, None] * fb)], axis=-1)
    h = jnp.sin(freq * (jnp.dot(z, w1, precision=hp) + b1))
    h = jnp.sin(freq * (jnp.dot(h, w2, precision=hp) + b2))
    h = jnp.sin(freq * (jnp.dot(h, w3, precision=hp) + b3))
    h = jnp.dot(h, w4, precision=hp).astype(F32).reshape(L, HYENA_ORDER, 2, HYENA_WIDTH)
    deltas = jnp.linspace(HYENA_MIN_DECAY, HYENA_MAX_DECAY, HYENA_WIDTH, dtype=F32)
    window = jnp.exp(-t[:, None] * jnp.abs(deltas))
    return h * window[:, None, None, :]


def _two_sided_long_conv(u, h_fwd, h_bwd, bias):
    B, L, C = u.shape
    filt2 = jnp.concatenate([h_fwd, jnp.zeros((1, C), F32), jnp.flip(h_bwd[1:], axis=0)], axis=0)
    uf = jnp.fft.rfft(u.astype(F32), n=2 * L, axis=1)
    ff = jnp.fft.rfft(filt2, n=2 * L, axis=0)
    y = jnp.fft.irfft(uf * ff[None], n=2 * L, axis=1)[:, :L]
    return (y + u.astype(F32) * bias.astype(F32)).astype(u.dtype)


def _hyena(xh, filt, bias):
    z = xh[:, :, 2]
    for o in range(HYENA_ORDER):
        z = xh[:, :, o] * _two_sided_long_conv(z, filt[:, o, 0], filt[:, o, 1], bias[o])
    return z


def _even_mixer_core(p, conv_w, a_log, dt_bias, gdn_gain, hf, hy_bias):
    def split(pp):
        B, L, _ = pp.shape
        cv = _depthwise_conv(pp[..., :N_CONV], conv_w)
        return (cv[..., :3 * GDN_WIDTH], cv[..., 3 * GDN_WIDTH:].reshape(B, L, 3, HYENA_WIDTH),
                pp[..., N_CONV:N_CONV + GDN_WIDTH], pp[..., N_CONV + GDN_WIDTH:])

    def gdn_out(o, z):
        B, H, L, dv = o.shape
        o = jnp.transpose(o, (0, 2, 1, 3)).astype(z.dtype)
        o = _rmsnorm(o, gdn_gain) * jax.nn.silu(z.reshape(B, L, H, dv))
        return o.reshape(B, L, H * dv)

    def hyena_branch(xh):
        return _hyena(xh, _hyena_filters(xh.shape[1], *hf), hy_bias)

    B = p.shape[0]
    qkv_c, hy_c, z_c, ba_c = split(p[:, :CTX_LEN])
    q, k, v, beta, g = _gdn_inputs(qkv_c, ba_c, a_log, dt_bias)
    s0 = jnp.zeros((2, B, GDN_HEADS, GDN_HEAD_DIM, GDN_HEAD_DIM), F32)
    o_c, s_c = _gdn_bidir(q, k, v, beta, g, s0)
    qkv_l, hy_l, z_l, ba_l = split(p[:, CTX_LEN:])
    q, k, v, beta, g = _gdn_inputs(qkv_l, ba_l, a_log, dt_bias)
    o_l, _ = _gdn_bidir(q, k, v, beta, g, s_c)
    a_lat = jnp.concatenate([gdn_out(o_l, z_l), hyena_branch(hy_l)], axis=-1)
    a_ctx = jnp.concatenate([gdn_out(o_c, z_c), hyena_branch(hy_c)], axis=-1)
    return jnp.concatenate([a_ctx, a_lat], axis=1)


SEL_GROUPS = 2 * PEER_HEADS
SEL_CANDS = -(-sum(PEER_TOPK // (a + 1) for a in range(PEER_TOPK)) // ROW_TILE) * ROW_TILE
_HIGHEST = lax.Precision.HIGHEST
_NT = (((1,), (1,)), ((), ()))


def _peer_select_kernel(q_ref, keys_ref, idx_ref, gate_ref, s_scr, sv_scr, si_scr, cand_scr, cidx_scr, val_scr):
    n_tok = PEER_TOKENS_PER_STEP
    row_f = lax.broadcasted_iota(jnp.int32, (PEER_KEYS, n_tok), 0).astype(F32)
    row2_f = lax.broadcasted_iota(jnp.int32, (SEL_CANDS, n_tok), 0).astype(F32)
    neg_inf = jnp.float32(-jnp.inf)

    for g in range(SEL_GROUPS):
        qg = q_ref[:, g * LANES:(g + 1) * LANES]
        s_scr[g] = lax.dot_general(keys_ref[g], qg, _NT, precision=_HIGHEST, preferred_element_type=F32)

    def key_pick(k, c):
        for g in range(SEL_GROUPS):
            s = s_scr[g]
            m = jnp.max(s, axis=0, keepdims=True)
            pos = jnp.min(jnp.where(s == m, row_f, float(PEER_KEYS)), axis=0, keepdims=True)
            s_scr[g] = jnp.where(row_f == pos, neg_inf, s)
            sv_scr[g, pl.ds(k, 1), :] = m
            si_scr[g, pl.ds(k, 1), :] = pos
        return c

    lax.fori_loop(0, PEER_TOPK, key_pick, 0)

    for h in range(PEER_HEADS):
        sv0, sv1 = sv_scr[2 * h], sv_scr[2 * h + 1]
        si0, si1 = si_scr[2 * h], si_scr[2 * h + 1]
        cand_scr[h] = jnp.full((SEL_CANDS, n_tok), neg_inf, F32)
        cidx_scr[h] = jnp.zeros((SEL_CANDS, n_tok), F32)
        r0 = 0
        for a in range(PEER_TOPK):
            nb = PEER_TOPK // (a + 1)
            cand_scr[h, r0:r0 + nb, :] = sv0[a:a + 1, :] + sv1[0:nb, :]
            cidx_scr[h, r0:r0 + nb, :] = si0[a:a + 1, :] * float(PEER_KEYS) + si1[0:nb, :]
            r0 += nb

    def expert_pick(k, c):
        for h in range(PEER_HEADS):
            cd = cand_scr[h]
            m = jnp.max(cd, axis=0, keepdims=True)
            pos = jnp.min(jnp.where(cd == m, row2_f, float(2 * SEL_CANDS)), axis=0, keepdims=True)
            hit = row2_f == pos
            e = jnp.max(jnp.where(hit, cidx_scr[h], -1.0), axis=0, keepdims=True)
            cand_scr[h] = jnp.where(hit, neg_inf, cd)
            val_scr[pl.ds(h * PEER_TOPK + k, 1), :] = m
            idx_ref[0, pl.ds(h * PEER_TOPK + k, 1), :] = e.astype(jnp.int32)
        return c

    lax.fori_loop(0, PEER_TOPK, expert_pick, 0)

    for h in range(PEER_HEADS):
        v = val_scr[h * PEER_TOPK:(h + 1) * PEER_TOPK, :]
        ex = jnp.exp(v - v[0:1, :])
        gate_ref[0, h * PEER_TOPK:(h + 1) * PEER_TOPK, :] = ex / jnp.sum(ex, axis=0, keepdims=True)


def peer_select(q, keys):
    n_steps = q.shape[0] // PEER_TOKENS_PER_STEP
    out_spec = pl.BlockSpec((1, PEER_PICKS, PEER_TOKENS_PER_STEP), lambda i: (i, 0, 0))
    scr = lambda n, r: pltpu.VMEM((n, r, PEER_TOKENS_PER_STEP), F32)
    return pl.pallas_call(
        _peer_select_kernel, grid=(n_steps,),
        in_specs=[pl.BlockSpec((PEER_TOKENS_PER_STEP, PEER_HEADS * PEER_QDIM), lambda i: (i, 0)),
                  pl.BlockSpec((SEL_GROUPS, PEER_KEYS, PEER_QDIM // 2), lambda i: (0, 0, 0))],
        out_specs=[out_spec, out_spec],
        out_shape=[jax.ShapeDtypeStruct((n_steps, PEER_PICKS, PEER_TOKENS_PER_STEP), jnp.int32),
                   jax.ShapeDtypeStruct((n_steps, PEER_PICKS, PEER_TOKENS_PER_STEP), F32)],
        scratch_shapes=[scr(SEL_GROUPS, PEER_KEYS), scr(SEL_GROUPS, PEER_TOPK), scr(SEL_GROUPS, PEER_TOPK),
                        scr(PEER_HEADS, SEL_CANDS), scr(PEER_HEADS, SEL_CANDS),
                        pltpu.VMEM((PEER_PICKS, PEER_TOKENS_PER_STEP), F32)],
        compiler_params=_cparams("parallel"), name="peer_select",
    )(q, keys.reshape(SEL_GROUPS, PEER_KEYS, PEER_QDIM // 2))


def pack_expert_table(tab):
    bits = lax.bitcast_convert_type(tab.astype(BF16), jnp.uint16).astype(jnp.uint32)
    bits = bits.reshape(N_EXPERTS // 2, 2, D_MODEL // LANES, LANES)
    return (bits[:, 0] | (bits[:, 1] << 16)).reshape(N_EXPERTS // 2 * ROW_TILE, LANES)


def _expert_row(tab_ref, offset, shift):
    w = tab_ref[pl.ds(pl.multiple_of(offset, ROW_TILE), ROW_TILE), :]
    return lax.bitcast_convert_type((w << shift.astype(jnp.uint32)) & jnp.uint32(0xFFFF0000), F32)


def _peer_act_kernel(off_ref, shift_ref, h_ref, tab_ref, act_ref):
    lane = lax.broadcasted_iota(jnp.int32, (ROW_TILE, LANES), 1)

    def token(t, carry):
        h = h_ref[t]
        acc = jnp.zeros((ROW_TILE, LANES), F32)
        for p in range(PEER_PICKS):
            s = p * PEER_TOKENS_PER_STEP + t
            row = _expert_row(tab_ref, off_ref[s], shift_ref[s])
            acc = jnp.where(lane == p, jnp.sum(row * h, axis=1, keepdims=True), acc)
        act_ref[pl.ds(t, 1), :] = jnp.sum(acc, axis=0, keepdims=True)
        return carry

    lax.fori_loop(0, PEER_TOKENS_PER_STEP, token, 0)


OUT_GROUP = 8
OUT_GROUPS = PEER_PICKS // OUT_GROUP
PAIR_ROWS = 2 * ROW_TILE


def pack_expert_rows(tab):
    return tab.astype(BF16).reshape(N_EXPERTS // 2, PAIR_ROWS, LANES)


def _peer_out_kernel(tile_ref, coef_ref, par_ref, expand_ref, x_ref, gate_ref, tab_ref, o_ref, ce_scr, pe_scr):
    ce_scr[...] = jnp.dot(coef_ref[...], expand_ref[...], precision=_HIGHEST, preferred_element_type=F32)
    pe_scr[...] = jnp.dot(par_ref[...], expand_ref[...], precision=_HIGHEST, preferred_element_type=F32)
    lane = lax.broadcasted_iota(jnp.int32, (ROW_TILE, LANES), 1)
    sub = lax.broadcasted_iota(jnp.int32, (ROW_TILE, LANES), 0)
    seg_hit = (lane % ROW_TILE) == sub
    lane_par = ((lane % PAIR_ROWS) // ROW_TILE).astype(F32)

    def octet(o, carry):
        t0 = pl.multiple_of(o * ROW_TILE, ROW_TILE)
        ce8 = ce_scr[pl.ds(t0, ROW_TILE), :]
        pe8 = pe_scr[pl.ds(t0, ROW_TILE), :]
        for r in range(ROW_TILE):
            t = t0 + r
            acc = jnp.zeros((PAIR_ROWS, LANES), F32)
            for g in range(OUT_GROUPS):
                tiles = [tab_ref[tile_ref[(g * OUT_GROUP + q) * PEER_TOKENS_PER_STEP + t]] for q in range(OUT_GROUP)]
                rhs = jnp.concatenate(tiles, axis=0)
                c = ce8[r:r + 1, g * LANES:(g + 1) * LANES]
                par = pe8[r:r + 1, g * LANES:(g + 1) * LANES]
                lhs = jnp.where(seg_hit & (lane_par == par), c, 0.0)
                hi = lhs.astype(BF16)
                lo = (lhs - hi.astype(F32)).astype(BF16)
                acc = acc + jnp.dot(jnp.concatenate([hi, lo], axis=0), rhs, preferred_element_type=F32)
            o_ref[t] = x_ref[t] + gate_ref[0, 0] * (acc[:ROW_TILE] + acc[ROW_TILE:])
        return carry

    lax.fori_loop(0, PEER_TOKENS_PER_STEP // ROW_TILE, octet, 0)


def _peer_specs():
    n_tok = BATCH * SEQ_ALL
    steps_per_batch = SEQ_ALL // PEER_TOKENS_PER_STEP
    ctx_steps = CTX_LEN // PEER_TOKENS_PER_STEP
    flat = PEER_TOKENS_PER_STEP * PEER_PICKS
    smem = pl.BlockSpec((flat,), lambda i: (i,), memory_space=pltpu.SMEM)
    tok = pl.BlockSpec((PEER_TOKENS_PER_STEP, ROW_TILE, LANES), lambda i: (i, 0, 0))
    tab = pl.BlockSpec((N_EXPERTS // 2 * ROW_TILE, LANES), lambda i: (0, 0), pipeline_mode=pl.Buffered(1))
    gate = pl.BlockSpec((1, 1, ROW_TILE, LANES),
                        lambda i: (i // steps_per_batch, jnp.minimum((i % steps_per_batch) // ctx_steps, 1), 0, 0))
    picks = pl.BlockSpec((PEER_TOKENS_PER_STEP, PEER_PICKS), lambda i: (i, 0))
    return n_tok, smem, tok, tab, gate, picks


def peer_activations(offset, shift, h, tab):
    n_tok, smem, tok, tabspec, _, picks = _peer_specs()
    return pl.pallas_call(
        _peer_act_kernel, grid=(n_tok // PEER_TOKENS_PER_STEP,),
        in_specs=[smem, smem, tok, tabspec], out_specs=picks,
        out_shape=jax.ShapeDtypeStruct((n_tok, PEER_PICKS), F32),
        compiler_params=_cparams("parallel"), name="peer_activations",
    )(offset, shift, h.reshape(n_tok, ROW_TILE, LANES), tab)


def peer_output_residual(tile, coef, parity, x, gate, tab):
    n_tok, smem, tok, _, gatespec, picks = _peer_specs()
    p = jnp.arange(PEER_PICKS)
    col = jnp.arange(OUT_GROUPS * LANES)
    expand = (p[:, None] == (col[None, :] // LANES) * OUT_GROUP + (col[None, :] % LANES) // PAIR_ROWS).astype(F32)
    wide = OUT_GROUPS * LANES
    out = pl.pallas_call(
        _peer_out_kernel, grid=(n_tok // PEER_TOKENS_PER_STEP,),
        in_specs=[smem, picks, picks, pl.BlockSpec((PEER_PICKS, wide), lambda i: (0, 0)), tok, gatespec,
                  pl.BlockSpec((N_EXPERTS // 2, PAIR_ROWS, LANES), lambda i: (0, 0, 0), pipeline_mode=pl.Buffered(1))],
        out_specs=tok,
        out_shape=jax.ShapeDtypeStruct((n_tok, ROW_TILE, LANES), F32),
        scratch_shapes=[pltpu.VMEM((PEER_TOKENS_PER_STEP, wide), F32), pltpu.VMEM((PEER_TOKENS_PER_STEP, wide), F32)],
        compiler_params=_cparams("parallel"), name="peer_output_residual",
    )(tile, coef, parity, expand, x.reshape(n_tok, ROW_TILE, LANES), gate.reshape(BATCH, 2, ROW_TILE, LANES), tab)
    return out.reshape(BATCH, SEQ_ALL, D_MODEL)


def kernel(x, c, ctx, c_ctx, norm1, norm2, w_ada, b_ada, w_in, conv_w, a_log, dt_bias, gdn_gain, hf_w1, hf_b1, hf_w2, hf_b2, hf_w3, hf_b3, hf_w4, hf_freq, hy_bias, w_out_even, w_qkv, lam_q1, lam_k1, lam_q2, lam_k2, subln, w_out_odd, peer_wq, peer_keys, peer_u, peer_v, final_norm):
    hp = lax.Precision.HIGHEST
    xs = jnp.concatenate([ctx, x], axis=1)
    cos_tab, sin_tab = rope_tables()
    cond = jnp.concatenate([jax.nn.silu(c_ctx)[None], jax.nn.silu(c)], axis=0)
    for layer in range(DEPTH):
        m = (jnp.dot(cond, w_ada[layer], precision=hp) + b_ada[layer]).reshape(BATCH + 1, 6, D_MODEL)
        m = jnp.stack([jnp.broadcast_to(m[0], (BATCH, 6, D_MODEL)), m[1:]], axis=1)
        mod1, gate1 = m[:, :, 0:2], m[:, :, 2:3]
        mod2, gate2 = m[:, :, 3:5], m[:, :, 5:6]
        if layer % 2 == 0:
            e = layer // 2
            w_pad = jnp.pad(w_in[e], ((0, 0), (0, N_IN_PAD - N_IN))).astype(BF16)
            p = norm_mod_proj(xs, norm1[layer], mod1, w_pad)[..., :N_IN]
            hf = (hf_w1[e], hf_b1[e], hf_w2[e], hf_b2[e], hf_w3[e], hf_b3[e], hf_w4[e], hf_freq[e])
            a = _even_mixer_core(p, conv_w[e], a_log[e], dt_bias[e], gdn_gain[e], hf, hy_bias[e])
            xs = out_proj_residual(a, w_out_even[e].astype(BF16), xs, gate1)
        else:
            o = layer // 2
            lam_init = 0.8 - 0.6 * math.exp(-0.3 * layer)
            lam = (jnp.exp(jnp.sum(lam_q1[o].astype(F32) * lam_k1[o].astype(F32)))
                   - jnp.exp(jnp.sum(lam_q2[o].astype(F32) * lam_k2[o].astype(F32))) + lam_init)
            qkv = norm_mod_proj_rope(xs, norm1[layer], mod1, w_qkv[o].astype(BF16), cos_tab, sin_tab)
            a = diff_attention(qkv, lam, subln[o], 1.0 - lam_init)
            xs = out_proj_residual(a, w_out_odd[o].astype(BF16), xs, gate1)
        h2, q = norm_mod_proj_h(xs, norm2[layer], mod2, peer_wq[layer].astype(BF16))
        idx, gate = peer_select(q.reshape(BATCH * SEQ_ALL, -1), peer_keys[layer])
        tile = idx >> 1
        offset = (tile * ROW_TILE).reshape(-1)
        shift = (16 - 16 * (idx & 1)).reshape(-1)
        act = peer_activations(offset, shift, h2, pack_expert_table(peer_u[layer]))
        by_token = lambda a: jnp.swapaxes(a, 1, 2).reshape(-1, PEER_PICKS)
        coef = by_token(gate) * jax.nn.gelu(act, approximate=False)
        xs = peer_output_residual(tile.reshape(-1), coef, by_token(idx & 1).astype(F32), xs, gate2,
                                  pack_expert_rows(peer_v[layer]))
    return _rmsnorm(xs[:, CTX_LEN:], final_norm)
```

```python
import functools
import math

import jax
import jax.numpy as jnp
from jax import lax
from jax.experimental import pallas as pl
from jax.experimental.pallas import tpu as pltpu

F32 = jnp.float32
BF16 = jnp.bfloat16

D_MODEL = 1024
BATCH = 8
SEQ = 4096
DEPTH = 4
GRID_W = 64
CTX_LEN = 256
SEQ_ALL = CTX_LEN + SEQ
ROW_BLOCK = CTX_LEN
N_ROW_BLOCKS = SEQ_ALL // ROW_BLOCK

GDN_HEAD_DIM = 128
GDN_WIDTH = D_MODEL // 2
GDN_HEADS = GDN_WIDTH // GDN_HEAD_DIM
GDN_CHUNK = 64
SHORT_CONV = 3

HYENA_WIDTH = D_MODEL // 2
HYENA_ORDER = 2
HYENA_EMB = 33
HYENA_DECAY_TARGET = 1e-2
HYENA_MIN_DECAY = math.log(HYENA_DECAY_TARGET) / 1.5
HYENA_MAX_DECAY = math.log(HYENA_DECAY_TARGET) / 0.3

N_CONV = 3 * GDN_WIDTH + 3 * HYENA_WIDTH
N_IN = N_CONV + GDN_WIDTH + 4 * GDN_HEADS
LANES = 128
N_IN_PAD = -(-N_IN // LANES) * LANES

DIFF_HEAD_DIM = 64
DIFF_HEADS = D_MODEL // (2 * DIFF_HEAD_DIM)
DIFF_V_DIM = 2 * DIFF_HEAD_DIM
DIFF_QK_WIDTH = DIFF_HEADS * 2 * DIFF_HEAD_DIM
DIFF_V_WIDTH = DIFF_HEADS * DIFF_V_DIM
ROPE_BASE = 10000.0
ROPE_FREQS = DIFF_HEAD_DIM // 4

PEER_HEADS = 8
PEER_KEYS = 128
PEER_TOPK = 16
PEER_QDIM = 256
PEER_TOKEN_BLOCK = 128
PEER_PICKS = PEER_HEADS * PEER_TOPK
N_EXPERTS = PEER_KEYS * PEER_KEYS
PEER_TOKENS_PER_STEP = 128
ROW_TILE = 8
NORM_EPS = 1e-6

VMEM_LIMIT = 56 * 1024 * 1024


def _cparams(*sem):
    return pltpu.CompilerParams(dimension_semantics=sem, vmem_limit_bytes=VMEM_LIMIT)


def _norm_mod(x, g, mod):
    y = x * lax.rsqrt(jnp.mean(x * x, axis=-1, keepdims=True) + NORM_EPS) * g
    return y * (1.0 + mod[1:2, :]) + mod[0:1, :]


def _proj_kernel(x_ref, g_ref, mod_ref, w_ref, o_ref):
    h = _norm_mod(x_ref[0], g_ref[...], mod_ref[0, 0])
    o_ref[0] = jnp.dot(h.astype(BF16), w_ref[...], preferred_element_type=F32).astype(o_ref.dtype)


def _proj_h_kernel(x_ref, g_ref, mod_ref, w_ref, h_ref, o_ref):
    h = _norm_mod(x_ref[0], g_ref[...], mod_ref[0, 0])
    h_ref[0] = h
    o_ref[0] = jnp.dot(h.astype(BF16), w_ref[...], preferred_element_type=F32)


def _proj_rope_kernel(x_ref, g_ref, mod_ref, w_ref, cos_ref, sin_ref, o_ref):
    h = _norm_mod(x_ref[0], g_ref[...], mod_ref[0, 0])
    p = jnp.dot(h.astype(BF16), w_ref[...], preferred_element_type=F32)
    cos = cos_ref[...]
    sin = sin_ref[...]
    lane = lax.broadcasted_iota(jnp.int32, (1, LANES), 1)
    first_half = (lane % (2 * ROPE_FREQS)) < ROPE_FREQS
    n_rot = 2 * DIFF_QK_WIDTH // LANES
    for j in range(p.shape[1] // LANES):
        blk = p[:, j * LANES:(j + 1) * LANES]
        if j < n_rot:
            partner = jnp.where(first_half, pltpu.roll(blk, LANES - ROPE_FREQS, 1),
                                pltpu.roll(blk, ROPE_FREQS, 1))
            blk = blk * cos + partner * sin
            if j < n_rot // 2:
                blk = blk * (DIFF_HEAD_DIM ** -0.5)
        o_ref[0, :, j * LANES:(j + 1) * LANES] = blk.astype(o_ref.dtype)


def _row_specs(n_out):
    x_spec = pl.BlockSpec((1, ROW_BLOCK, D_MODEL), lambda b, s: (b, s, 0))
    g_spec = pl.BlockSpec((1, D_MODEL), lambda b, s: (0, 0))
    mod_spec = pl.BlockSpec((1, 1, 2, D_MODEL), lambda b, s: (b, jnp.minimum(s, 1), 0, 0))
    w_spec = pl.BlockSpec((D_MODEL, n_out), lambda b, s: (0, 0))
    o_spec = pl.BlockSpec((1, ROW_BLOCK, n_out), lambda b, s: (b, s, 0))
    return x_spec, g_spec, mod_spec, w_spec, o_spec


def norm_mod_proj(x, g, mod, w, out_dtype=F32):
    n_out = w.shape[1]
    x_spec, g_spec, mod_spec, w_spec, o_spec = _row_specs(n_out)
    return pl.pallas_call(
        _proj_kernel, grid=(BATCH, N_ROW_BLOCKS),
        in_specs=[x_spec, g_spec, mod_spec, w_spec], out_specs=o_spec,
        out_shape=jax.ShapeDtypeStruct((BATCH, SEQ_ALL, n_out), out_dtype),
        compiler_params=_cparams("parallel", "parallel"), name="norm_mod_proj",
    )(x, g.reshape(1, D_MODEL), mod, w)


def norm_mod_proj_h(x, g, mod, w):
    n_out = w.shape[1]
    x_spec, g_spec, mod_spec, w_spec, o_spec = _row_specs(n_out)
    return pl.pallas_call(
        _proj_h_kernel, grid=(BATCH, N_ROW_BLOCKS),
        in_specs=[x_spec, g_spec, mod_spec, w_spec], out_specs=[x_spec, o_spec],
        out_shape=[jax.ShapeDtypeStruct((BATCH, SEQ_ALL, D_MODEL), F32),
                   jax.ShapeDtypeStruct((BATCH, SEQ_ALL, n_out), F32)],
        compiler_params=_cparams("parallel", "parallel"), name="norm_mod_proj_h",
    )(x, g.reshape(1, D_MODEL), mod, w)


def norm_mod_proj_rope(x, g, mod, w, cos_tab, sin_tab):
    n_out = w.shape[1]
    x_spec, g_spec, mod_spec, w_spec, o_spec = _row_specs(n_out)
    tab_spec = pl.BlockSpec((ROW_BLOCK, LANES), lambda b, s: (s, 0))
    return pl.pallas_call(
        _proj_rope_kernel, grid=(BATCH, N_ROW_BLOCKS),
        in_specs=[x_spec, g_spec, mod_spec, w_spec, tab_spec, tab_spec], out_specs=o_spec,
        out_shape=jax.ShapeDtypeStruct((BATCH, SEQ_ALL, n_out), BF16),
        compiler_params=_cparams("parallel", "parallel"), name="norm_mod_proj_rope",
    )(x, g.reshape(1, D_MODEL), mod, w, cos_tab, sin_tab)


def _out_proj_kernel(a_ref, w_ref, x_ref, gate_ref, o_ref):
    y = jnp.dot(a_ref[0].astype(BF16), w_ref[...], preferred_element_type=F32)
    o_ref[0] = x_ref[0] + gate_ref[0, 0] * y


def out_proj_residual(a, w, x, gate):
    k = a.shape[-1]
    return pl.pallas_call(
        _out_proj_kernel, grid=(BATCH, N_ROW_BLOCKS),
        in_specs=[pl.BlockSpec((1, ROW_BLOCK, k), lambda b, s: (b, s, 0)),
                  pl.BlockSpec((k, D_MODEL), lambda b, s: (0, 0)),
                  pl.BlockSpec((1, ROW_BLOCK, D_MODEL), lambda b, s: (b, s, 0)),
                  pl.BlockSpec((1, 1, 1, D_MODEL), lambda b, s: (b, jnp.minimum(s, 1), 0, 0))],
        out_specs=pl.BlockSpec((1, ROW_BLOCK, D_MODEL), lambda b, s: (b, s, 0)),
        out_shape=jax.ShapeDtypeStruct((BATCH, SEQ_ALL, D_MODEL), F32),
        compiler_params=_cparams("parallel", "parallel"), name="out_proj_residual",
    )(a, w, x, gate)


def _diff_attn_kernel(lam_ref, q_ref, k_ref, v_ref, gain_ref, o_ref, *, out_scale):
    lam = lam_ref[0]
    lane = lax.broadcasted_iota(jnp.int32, (1, LANES), 1)
    in_map0 = lane < DIFF_HEAD_DIM

    def attend(n_keys):
        q = q_ref[0]
        k = k_ref[0, :n_keys, :]
        v = v_ref[0, :n_keys, :]
        zero = jnp.zeros_like(q)
        nt = (((1,), (1,)), ((), ()))
        s0 = lax.dot_general(jnp.where(in_map0, q, zero), k, nt, preferred_element_type=F32)
        s1 = lax.dot_general(jnp.where(in_map0, zero, q), k, nt, preferred_element_type=F32)
        p0 = jnp.exp(s0 - jnp.max(s0, axis=-1, keepdims=True))
        p1 = jnp.exp(s1 - jnp.max(s1, axis=-1, keepdims=True))
        r0 = 1.0 / jnp.sum(p0, axis=-1, keepdims=True)
        r1 = lam / jnp.sum(p1, axis=-1, keepdims=True)
        a = p0 * r0 - p1 * r1
        o = jnp.dot(a.astype(BF16), v, preferred_element_type=F32)
        o = o * lax.rsqrt(jnp.mean(o * o, axis=-1, keepdims=True) + NORM_EPS)
        o_ref[0] = (o * gain_ref[...] * out_scale).astype(o_ref.dtype)

    is_ctx = pl.program_id(2) == 0

    @pl.when(is_ctx)
    def _():
        attend(CTX_LEN)

    @pl.when(jnp.logical_not(is_ctx))
    def _():
        attend(SEQ_ALL)


def diff_attention(qkv, lam, gain, out_scale):
    kv_blocks = DIFF_QK_WIDTH // LANES
    return pl.pallas_call(
        functools.partial(_diff_attn_kernel, out_scale=out_scale),
        grid=(BATCH, DIFF_HEADS, N_ROW_BLOCKS),
        in_specs=[pl.BlockSpec(memory_space=pltpu.SMEM),
                  pl.BlockSpec((1, ROW_BLOCK, LANES), lambda b, h, s: (b, s, h)),
                  pl.BlockSpec((1, SEQ_ALL, LANES), lambda b, h, s: (b, 0, kv_blocks + h)),
                  pl.BlockSpec((1, SEQ_ALL, LANES), lambda b, h, s: (b, 0, 2 * kv_blocks + h)),
                  pl.BlockSpec((1, LANES), lambda b, h, s: (0, 0))],
        out_specs=pl.BlockSpec((1, ROW_BLOCK, LANES), lambda b, h, s: (b, s, h)),
        out_shape=jax.ShapeDtypeStruct((BATCH, SEQ_ALL, DIFF_V_WIDTH), BF16),
        compiler_params=_cparams("parallel", "parallel", "arbitrary"), name="diff_attention",
    )(lam.reshape(1), qkv, qkv, qkv, gain.reshape(1, LANES))


def rope_tables():
    t = jnp.arange(SEQ)
    pos = jnp.stack([t // GRID_W, t % GRID_W], axis=-1).astype(F32)
    inv = ROPE_BASE ** (-jnp.arange(ROPE_FREQS, dtype=F32) / ROPE_FREQS)
    ang = pos[:, :, None] * inv
    cos = jnp.cos(ang)
    sin = jnp.sin(ang)
    cos_l = jnp.tile(jnp.concatenate([cos, cos], axis=-1).reshape(SEQ, 4 * ROPE_FREQS), (1, 2))
    sin_l = jnp.tile(jnp.concatenate([-sin, sin], axis=-1).reshape(SEQ, 4 * ROPE_FREQS), (1, 2))
    cos_t = jnp.concatenate([jnp.ones((CTX_LEN, LANES), F32), cos_l], axis=0)
    sin_t = jnp.concatenate([jnp.zeros((CTX_LEN, LANES), F32), sin_l], axis=0)
    return cos_t, sin_t


def _rmsnorm(x, g):
    xf = x.astype(F32)
    y = xf * lax.rsqrt(jnp.mean(xf * xf, axis=-1, keepdims=True) + NORM_EPS)
    return (y * g.astype(F32)).astype(x.dtype)


def _l2norm(x):
    x = x.astype(F32)
    return x * lax.rsqrt(jnp.sum(x * x, axis=-1, keepdims=True) + NORM_EPS)


def _depthwise_conv(x, w):
    return lax.conv_general_dilated(x, w[:, None, :].astype(x.dtype), window_strides=(1,),
                                    padding=[(SHORT_CONV // 2, SHORT_CONV // 2)],
                                    dimension_numbers=('NWC', 'WIO', 'NWC'),
                                    feature_group_count=x.shape[-1])


def _gated_delta_chunked(q, k, v, beta, g, s0, reverse=False):
    B, H, L, dk = q.shape
    dv = v.shape[-1]
    C = GDN_CHUNK
    n = L // C
    hp = lax.Precision.HIGHEST
    q, k, v, beta, g = [t.astype(F32).reshape(B, H, n, C, *t.shape[3:]) for t in (q, k, v, beta, g)]
    G = jnp.cumsum(g, axis=-1)
    causal = jnp.tril(jnp.ones((C, C), bool))
    strict = jnp.tril(jnp.ones((C, C), bool), -1)
    gdiff = G[..., :, None] - G[..., None, :]
    decay = jnp.where(causal, jnp.exp(jnp.where(causal, gdiff, 0.0)), 0.0)
    kk = jnp.einsum('bhncd,bhnjd->bhncj', k, k, precision=hp)
    a_mat = jnp.where(strict, beta[..., :, None] * kk * decay, 0.0)
    eye = jnp.eye(C, dtype=F32)
    rhs = jnp.concatenate([beta[..., None] * v, (beta * jnp.exp(G))[..., None] * k], axis=-1)
    sol = lax.linalg.triangular_solve(eye + a_mat, rhs, left_side=True, lower=True, unit_diagonal=True)
    w_intra, k_cum = sol[..., :dv], sol[..., dv:]
    qk = jnp.einsum('bhncd,bhnjd->bhncj', q, k, precision=hp) * decay
    q_dec = q * jnp.exp(G)[..., None]
    k_dec = k * jnp.exp(G[..., -1:] - G)[..., None]
    chunk_decay = jnp.exp(G[..., -1])

    def step(s, inp):
        w_i, kc_i, qk_i, qd_i, kd_i, cd_i = inp
        w = w_i - jnp.einsum('bhcd,bhde->bhce', kc_i, s, precision=hp)
        o = (jnp.einsum('bhcd,bhde->bhce', qd_i, s, precision=hp)
             + jnp.einsum('bhcj,bhje->bhce', qk_i, w, precision=hp))
        s = s * cd_i[..., None, None] + jnp.einsum('bhcd,bhce->bhde', kd_i, w, precision=hp)
        return s, o

    xs = tuple(jnp.moveaxis(t, 2, 0) for t in (w_intra, k_cum, qk, q_dec, k_dec, chunk_decay))
    s_fin, o = lax.scan(step, s0.astype(F32), xs, reverse=reverse)
    o = jnp.moveaxis(o, 0, 2).reshape(B, H, L, dv)
    return o, s_fin


def _gdn_inputs(qkv, ba, a_log, dt_bias):
    B, L, _ = qkv.shape
    qkv = jnp.transpose(jax.nn.silu(qkv).reshape(B, L, 3, GDN_HEADS, GDN_HEAD_DIM), (2, 0, 3, 1, 4))
    q = _l2norm(qkv[0]) * (GDN_HEAD_DIM ** -0.5)
    k = _l2norm(qkv[1])
    v = qkv[2]
    ba = ba.astype(F32).reshape(B, L, 2, 2, GDN_HEADS)
    beta = jax.nn.sigmoid(ba[:, :, 0])
    g = -jnp.exp(a_log.astype(F32)) * jax.nn.softplus(ba[:, :, 1] + dt_bias.astype(F32))
    return q, k, v, jnp.transpose(beta, (2, 0, 3, 1)), jnp.transpose(g, (2, 0, 3, 1))


def _gdn_bidir(q, k, v, beta, g, s0):
    o_f, s_f = _gated_delta_chunked(q, k, v, beta[0], g[0], s0[0])
    anti_eye = jnp.flip(jnp.eye(GDN_CHUNK, dtype=F32), axis=0)

    def fl_rows(t):
        B, H, L, D = t.shape
        t = t.reshape(B, H, L // GDN_CHUNK, GDN_CHUNK, D)
        return jnp.einsum('ij,bhnjd->bhnid', anti_eye, t, precision=lax.Precision.HIGHEST).reshape(B, H, L, D)

    def fl_scalars(t):
        B, H, L = t.shape
        return jnp.flip(t.reshape(B, H, L // GDN_CHUNK, GDN_CHUNK), axis=3).reshape(B, H, L)

    o_b, s_b = _gated_delta_chunked(fl_rows(q), fl_rows(k), fl_rows(v), fl_scalars(beta[1]), fl_scalars(g[1]),
                                    s0[1], reverse=True)
    return o_f + fl_rows(o_b), jnp.stack([s_f, s_b])


def _hyena_filters(L, w1, b1, w2, b2, w3, b3, w4, freq):
    hp = lax.Precision.HIGHEST
    t = jnp.linspace(0.0, 1.0, L, dtype=F32)
    bands = (HYENA_EMB - 1) // 2
    wpos = 2.0 * math.pi * jnp.arange(L, dtype=F32) / L
    fb = jnp.linspace(1e-4, bands - 1, bands, dtype=F32)
    z = jnp.concatenate([t[:, None], jnp.cos(wpos[:, None] * fb), -jnp.sin(wpos[:, None] * fb)], axis=-1)
    h = jnp.sin(freq * (jnp.dot(z, w1, precision=hp) + b1))
    h = jnp.sin(freq * (jnp.dot(h, w2, precision=hp) + b2))
    h = jnp.sin(freq * (jnp.dot(h, w3, precision=hp) + b3))
    h = jnp.dot(h, w4, precision=hp).astype(F32).reshape(L, HYENA_ORDER, 2, HYENA_WIDTH)
    deltas = jnp.linspace(HYENA_MIN_DECAY, HYENA_MAX_DECAY, HYENA_WIDTH, dtype=F32)
    window = jnp.exp(-t[:, None] * jnp.abs(deltas))
    return h * window[:, None, None, :]


def _two_sided_long_conv(u, h_fwd, h_bwd, bias):
    B, L, C = u.shape
    filt2 = jnp.concatenate([h_fwd, jnp.zeros((1, C), F32), jnp.flip(h_bwd[1:], axis=0)], axis=0)
    uf = jnp.fft.rfft(u.astype(F32), n=2 * L, axis=1)
    ff = jnp.fft.rfft(filt2, n=2 * L, axis=0)
    y = jnp.fft.irfft(uf * ff[None], n=2 * L, axis=1)[:, :L]
    return (y + u.astype(F32) * bias.astype(F32)).astype(u.dtype)


def _hyena(xh, filt, bias):
    z = xh[:, :, 2]
    for o in range(HYENA_ORDER):
        z = xh[:, :, o] * _two_sided_long_conv(z, filt[:, o, 0], filt[:, o, 1], bias[o])
    return z


def _even_mixer_core(p, conv_w, a_log, dt_bias, gdn_gain, hf, hy_bias):
    def split(pp):
        B, L, _ = pp.shape
        cv = _depthwise_conv(pp[..., :N_CONV], conv_w)
        return (cv[..., :3 * GDN_WIDTH], cv[..., 3 * GDN_WIDTH:].reshape(B, L, 3, HYENA_WIDTH),
                pp[..., N_CONV:N_CONV + GDN_WIDTH], pp[..., N_CONV + GDN_WIDTH:])

    def gdn_out(o, z):
        B, H, L, dv = o.shape
        o = jnp.transpose(o, (0, 2, 1, 3)).astype(z.dtype)
        o = _rmsnorm(o, gdn_gain) * jax.nn.silu(z.reshape(B, L, H, dv))
        return o.reshape(B, L, H * dv)

    def hyena_branch(xh):
        return _hyena(xh, _hyena_filters(xh.shape[1], *hf), hy_bias)

    B = p.shape[0]
    qkv_c, hy_c, z_c, ba_c = split(p[:, :CTX_LEN])
    q, k, v, beta, g = _gdn_inputs(qkv_c, ba_c, a_log, dt_bias)
    s0 = jnp.zeros((2, B, GDN_HEADS, GDN_HEAD_DIM, GDN_HEAD_DIM), F32)
    o_c, s_c = _gdn_bidir(q, k, v, beta, g, s0)
    qkv_l, hy_l, z_l, ba_l = split(p[:, CTX_LEN:])
    q, k, v, beta, g = _gdn_inputs(qkv_l, ba_l, a_log, dt_bias)
    o_l, _ = _gdn_bidir(q, k, v, beta, g, s_c)
    a_lat = jnp.concatenate([gdn_out(o_l, z_l), hyena_branch(hy_l)], axis=-1)
    a_ctx = jnp.concatenate([gdn_out(o_c, z_c), hyena_branch(hy_c)], axis=-1)
    return jnp.concatenate([a_ctx, a_lat], axis=1)


SEL_GROUPS = 2 * PEER_HEADS
SEL_CANDS = -(-sum(PEER_TOPK // (a + 1) for a in range(PEER_TOPK)) // ROW_TILE) * ROW_TILE
_HIGHEST = lax.Precision.HIGHEST
_NT = (((1,), (1,)), ((), ()))


def _peer_select_kernel(q_ref, keys_ref, idx_ref, gate_ref, s_scr, sv_scr, si_scr, cand_scr, cidx_scr, val_scr):
    n_tok = PEER_TOKENS_PER_STEP
    row_f = lax.broadcasted_iota(jnp.int32, (PEER_KEYS, n_tok), 0).astype(F32)
    row2_f = lax.broadcasted_iota(jnp.int32, (SEL_CANDS, n_tok), 0).astype(F32)
    neg_inf = jnp.float32(-jnp.inf)

    for g in range(SEL_GROUPS):
        qg = q_ref[:, g * LANES:(g + 1) * LANES]
        s_scr[g] = lax.dot_general(keys_ref[g], qg, _NT, precision=_HIGHEST, preferred_element_type=F32)

    def key_pick(k, c):
        for g in range(SEL_GROUPS):
            s = s_scr[g]
            m = jnp.max(s, axis=0, keepdims=True)
            pos = jnp.min(jnp.where(s == m, row_f, float(PEER_KEYS)), axis=0, keepdims=True)
            s_scr[g] = jnp.where(row_f == pos, neg_inf, s)
            sv_scr[g, pl.ds(k, 1), :] = m
            si_scr[g, pl.ds(k, 1), :] = pos
        return c

    lax.fori_loop(0, PEER_TOPK, key_pick, 0)

    for h in range(PEER_HEADS):
        sv0, sv1 = sv_scr[2 * h], sv_scr[2 * h + 1]
        si0, si1 = si_scr[2 * h], si_scr[2 * h + 1]
        cand_scr[h] = jnp.full((SEL_CANDS, n_tok), neg_inf, F32)
        cidx_scr[h] = jnp.zeros((SEL_CANDS, n_tok), F32)
        r0 = 0
        for a in range(PEER_TOPK):
            nb = PEER_TOPK // (a + 1)
            cand_scr[h, r0:r0 + nb, :] = sv0[a:a + 1, :] + sv1[0:nb, :]
            cidx_scr[h, r0:r0 + nb, :] = si0[a:a + 1, :] * float(PEER_KEYS) + si1[0:nb, :]
            r0 += nb

    def expert_pick(k, c):
        for h in range(PEER_HEADS):
            cd = cand_scr[h]
            m = jnp.max(cd, axis=0, keepdims=True)
            pos = jnp.min(jnp.where(cd == m, row2_f, float(2 * SEL_CANDS)), axis=0, keepdims=True)
            hit = row2_f == pos
            e = jnp.max(jnp.where(hit, cidx_scr[h], -1.0), axis=0, keepdims=True)
            cand_scr[h] = jnp.where(hit, neg_inf, cd)
            val_scr[pl.ds(h * PEER_TOPK + k, 1), :] = m
            idx_ref[0, pl.ds(h * PEER_TOPK + k, 1), :] = e.astype(jnp.int32)
        return c

    lax.fori_loop(0, PEER_TOPK, expert_pick, 0)

    for h in range(PEER_HEADS):
        v = val_scr[h * PEER_TOPK:(h + 1) * PEER_TOPK, :]
        ex = jnp.exp(v - v[0:1, :])
        gate_ref[0, h * PEER_TOPK:(h + 1) * PEER_TOPK, :] = ex / jnp.sum(ex, axis=0, keepdims=True)


def peer_select(q, keys):
    n_steps = q.shape[0] // PEER_TOKENS_PER_STEP
    out_spec = pl.BlockSpec((1, PEER_PICKS, PEER_TOKENS_PER_STEP), lambda i: (i, 0, 0))
    scr = lambda n, r: pltpu.VMEM((n, r, PEER_TOKENS_PER_STEP), F32)
    return pl.pallas_call(
        _peer_select_kernel, grid=(n_steps,),
        in_specs=[pl.BlockSpec((PEER_TOKENS_PER_STEP, PEER_HEADS * PEER_QDIM), lambda i: (i, 0)),
                  pl.BlockSpec((SEL_GROUPS, PEER_KEYS, PEER_QDIM // 2), lambda i: (0, 0, 0))],
        out_specs=[out_spec, out_spec],
        out_shape=[jax.ShapeDtypeStruct((n_steps, PEER_PICKS, PEER_TOKENS_PER_STEP), jnp.int32),
                   jax.ShapeDtypeStruct((n_steps, PEER_PICKS, PEER_TOKENS_PER_STEP), F32)],
        scratch_shapes=[scr(SEL_GROUPS, PEER_KEYS), scr(SEL_GROUPS, PEER_TOPK), scr(SEL_GROUPS, PEER_TOPK),
                        scr(PEER_HEADS, SEL_CANDS), scr(PEER_HEADS, SEL_CANDS),
                        pltpu.VMEM((PEER_PICKS, PEER_TOKENS_PER_STEP), F32)],
        compiler_params=_cparams("parallel"), name="peer_select",
    )(q, keys.reshape(SEL_GROUPS, PEER_KEYS, PEER_QDIM // 2))


def pack_expert_table(tab):
    bits = lax.bitcast_convert_type(tab.astype(BF16), jnp.uint16).astype(jnp.uint32)
    bits = bits.reshape(N_EXPERTS // 2, 2, D_MODEL // LANES, LANES)
    return (bits[:, 0] | (bits[:, 1] << 16)).reshape(N_EXPERTS // 2 * ROW_TILE, LANES)


def _expert_row(tab_ref, offset, shift):
    w = tab_ref[pl.ds(pl.multiple_of(offset, ROW_TILE), ROW_TILE), :]
    return lax.bitcast_convert_type((w << shift.astype(jnp.uint32)) & jnp.uint32(0xFFFF0000), F32)


def _peer_act_kernel(off_ref, shift_ref, h_ref, tab_ref, act_ref):
    lane = lax.broadcasted_iota(jnp.int32, (ROW_TILE, LANES), 1)

    def token(t, carry):
        h = h_ref[t]
        acc = jnp.zeros((ROW_TILE, LANES), F32)
        for p in range(PEER_PICKS):
            s = p * PEER_TOKENS_PER_STEP + t
            row = _expert_row(tab_ref, off_ref[s], shift_ref[s])
            acc = jnp.where(lane == p, jnp.sum(row * h, axis=1, keepdims=True), acc)
        act_ref[pl.ds(t, 1), :] = jnp.sum(acc, axis=0, keepdims=True)
        return carry

    lax.fori_loop(0, PEER_TOKENS_PER_STEP, token, 0, unroll=8)


OUT_GROUP = 8
OUT_GROUPS = PEER_PICKS // OUT_GROUP
PAIR_ROWS = 2 * ROW_TILE


def pack_expert_rows(tab):
    return tab.astype(BF16).reshape(N_EXPERTS // 2, PAIR_ROWS, LANES)


def _peer_out_kernel(tile_ref, coef_ref, par_ref, expand_ref, x_ref, gate_ref, tab_ref, o_ref, ce_scr, pe_scr):
    ce_scr[...] = jnp.dot(coef_ref[...], expand_ref[...], precision=_HIGHEST, preferred_element_type=F32)
    pe_scr[...] = jnp.dot(par_ref[...], expand_ref[...], precision=_HIGHEST, preferred_element_type=F32)
    lane = lax.broadcasted_iota(jnp.int32, (ROW_TILE, LANES), 1)
    sub = lax.broadcasted_iota(jnp.int32, (ROW_TILE, LANES), 0)
    seg_hit = (lane % ROW_TILE) == sub
    lane_par = ((lane % PAIR_ROWS) // ROW_TILE).astype(F32)

    def octet(o, carry):
        t0 = pl.multiple_of(o * ROW_TILE, ROW_TILE)
        ce8 = ce_scr[pl.ds(t0, ROW_TILE), :]
        pe8 = pe_scr[pl.ds(t0, ROW_TILE), :]
        for r in range(ROW_TILE):
            t = t0 + r
            acc = jnp.zeros((PAIR_ROWS, LANES), F32)
            for g in range(OUT_GROUPS):
                tiles = [tab_ref[tile_ref[(g * OUT_GROUP + q) * PEER_TOKENS_PER_STEP + t]] for q in range(OUT_GROUP)]
                rhs = jnp.concatenate(tiles, axis=0)
                c = ce8[r:r + 1, g * LANES:(g + 1) * LANES]
                par = pe8[r:r + 1, g * LANES:(g + 1) * LANES]
                lhs = jnp.where(seg_hit & (lane_par == par), c, 0.0)
                hi = lhs.astype(BF16)
                lo = (lhs - hi.astype(F32)).astype(BF16)
                acc = acc + jnp.dot(jnp.concatenate([hi, lo], axis=0), rhs, preferred_element_type=F32)
            o_ref[t] = x_ref[t] + gate_ref[0, 0] * (acc[:ROW_TILE] + acc[ROW_TILE:])
        return carry

    lax.fori_loop(0, PEER_TOKENS_PER_STEP // ROW_TILE, octet, 0)


def _peer_specs():
    n_tok = BATCH * SEQ_ALL
    steps_per_batch = SEQ_ALL // PEER_TOKENS_PER_STEP
    ctx_steps = CTX_LEN // PEER_TOKENS_PER_STEP
    flat = PEER_TOKENS_PER_STEP * PEER_PICKS
    smem = pl.BlockSpec((flat,), lambda i: (i,), memory_space=pltpu.SMEM)
    tok = pl.BlockSpec((PEER_TOKENS_PER_STEP, ROW_TILE, LANES), lambda i: (i, 0, 0))
    tab = pl.BlockSpec((N_EXPERTS // 2 * ROW_TILE, LANES), lambda i: (0, 0), pipeline_mode=pl.Buffered(1))
    gate = pl.BlockSpec((1, 1, ROW_TILE, LANES),
                        lambda i: (i // steps_per_batch, jnp.minimum((i % steps_per_batch) // ctx_steps, 1), 0, 0))
    picks = pl.BlockSpec((PEER_TOKENS_PER_STEP, PEER_PICKS), lambda i: (i, 0))
    return n_tok, smem, tok, tab, gate, picks


def peer_activations(offset, shift, h, tab):
    n_tok, smem, tok, tabspec, _, picks = _peer_specs()
    return pl.pallas_call(
        _peer_act_kernel, grid=(n_tok // PEER_TOKENS_PER_STEP,),
        in_specs=[smem, smem, tok, tabspec], out_specs=picks,
        out_shape=jax.ShapeDtypeStruct((n_tok, PEER_PICKS), F32),
        compiler_params=_cparams("parallel"), name="peer_activations",
    )(offset, shift, h.reshape(n_tok, ROW_TILE, LANES), tab)


def peer_output_residual(tile, coef, parity, x, gate, tab):
    n_tok, smem, tok, _, gatespec, picks = _peer_specs()
    p = jnp.arange(PEER_PICKS)
    col = jnp.arange(OUT_GROUPS * LANES)
    expand = (p[:, None] == (col[None, :] // LANES) * OUT_GROUP + (col[None, :] % LANES) // PAIR_ROWS).astype(F32)
    wide = OUT_GROUPS * LANES
    out = pl.pallas_call(
        _peer_out_kernel, grid=(n_tok // PEER_TOKENS_PER_STEP,),
        in_specs=[smem, picks, picks, pl.BlockSpec((PEER_PICKS, wide), lambda i: (0, 0)), tok, gatespec,
                  pl.BlockSpec((N_EXPERTS // 2, PAIR_ROWS, LANES), lambda i: (0, 0, 0), pipeline_mode=pl.Buffered(1))],
        out_specs=tok,
        out_shape=jax.ShapeDtypeStruct((n_tok, ROW_TILE, LANES), F32),
        scratch_shapes=[pltpu.VMEM((PEER_TOKENS_PER_STEP, wide), F32), pltpu.VMEM((PEER_TOKENS_PER_STEP, wide), F32)],
        compiler_params=_cparams("parallel"), name="peer_output_residual",
    )(tile, coef, parity, expand, x.reshape(n_tok, ROW_TILE, LANES), gate.reshape(BATCH, 2, ROW_TILE, LANES), tab)
    return out.reshape(BATCH, SEQ_ALL, D_MODEL)


def kernel(x, c, ctx, c_ctx, norm1, norm2, w_ada, b_ada, w_in, conv_w, a_log, dt_bias, gdn_gain, hf_w1, hf_b1, hf_w2, hf_b2, hf_w3, hf_b3, hf_w4, hf_freq, hy_bias, w_out_even, w_qkv, lam_q1, lam_k1, lam_q2, lam_k2, subln, w_out_odd, peer_wq, peer_keys, peer_u, peer_v, final_norm):
    hp = lax.Precision.HIGHEST
    xs = jnp.concatenate([ctx, x], axis=1)
    cos_tab, sin_tab = rope_tables()
    cond = jnp.concatenate([jax.nn.silu(c_ctx)[None], jax.nn.silu(c)], axis=0)
    for layer in range(DEPTH):
        m = (jnp.dot(cond, w_ada[layer], precision=hp) + b_ada[layer]).reshape(BATCH + 1, 6, D_MODEL)
        m = jnp.stack([jnp.broadcast_to(m[0], (BATCH, 6, D_MODEL)), m[1:]], axis=1)
        mod1, gate1 = m[:, :, 0:2], m[:, :, 2:3]
        mod2, gate2 = m[:, :, 3:5], m[:, :, 5:6]
        if layer % 2 == 0:
            e = layer // 2
            w_pad = jnp.pad(w_in[e], ((0, 0), (0, N_IN_PAD - N_IN))).astype(BF16)
            p = norm_mod_proj(xs, norm1[layer], mod1, w_pad)[..., :N_IN]
            hf = (hf_w1[e], hf_b1[e], hf_w2[e], hf_b2[e], hf_w3[e], hf_b3[e], hf_w4[e], hf_freq[e])
            a = _even_mixer_core(p, conv_w[e], a_log[e], dt_bias[e], gdn_gain[e], hf, hy_bias[e])
            xs = out_proj_residual(a, w_out_even[e].astype(BF16), xs, gate1)
        else:
            o = layer // 2
            lam_init = 0.8 - 0.6 * math.exp(-0.3 * layer)
            lam = (jnp.exp(jnp.sum(lam_q1[o].astype(F32) * lam_k1[o].astype(F32)))
                   - jnp.exp(jnp.sum(lam_q2[o].astype(F32) * lam_k2[o].astype(F32))) + lam_init)
            qkv = norm_mod_proj_rope(xs, norm1[layer], mod1, w_qkv[o].astype(BF16), cos_tab, sin_tab)
            a = diff_attention(qkv, lam, subln[o], 1.0 - lam_init)
            xs = out_proj_residual(a, w_out_odd[o].astype(BF16), xs, gate1)
        h2, q = norm_mod_proj_h(xs, norm2[layer], mod2, peer_wq[layer].astype(BF16))
        idx, gate = peer_select(q.reshape(BATCH * SEQ_ALL, -1), peer_keys[layer])
        tile = idx >> 1
        offset = (tile * ROW_TILE).reshape(-1)
        shift = (16 - 16 * (idx & 1)).reshape(-1)
        act = peer_activations(offset, shift, h2, pack_expert_table(peer_u[layer]))
        by_token = lambda a: jnp.swapaxes(a, 1, 2).reshape(-1, PEER_PICKS)
        coef = by_token(gate) * jax.nn.gelu(act, approximate=False)
        xs = peer_output_residual(tile.reshape(-1), coef, by_token(idx & 1).astype(F32), xs, gate2,
                                  pack_expert_rows(peer_v[layer]))
    return _rmsnorm(xs[:, CTX_LEN:], final_norm)
```

```python
import functools
import math

import jax
import jax.numpy as jnp
from jax import lax
from jax.experimental import pallas as pl
from jax.experimental.pallas import tpu as pltpu

F32 = jnp.float32
BF16 = jnp.bfloat16

D_MODEL = 1024
BATCH = 8
SEQ = 4096
DEPTH = 4
GRID_W = 64
CTX_LEN = 256
SEQ_ALL = CTX_LEN + SEQ
ROW_BLOCK = CTX_LEN
N_ROW_BLOCKS = SEQ_ALL // ROW_BLOCK

GDN_HEAD_DIM = 128
GDN_WIDTH = D_MODEL // 2
GDN_HEADS = GDN_WIDTH // GDN_HEAD_DIM
GDN_CHUNK = 64
SHORT_CONV = 3

HYENA_WIDTH = D_MODEL // 2
HYENA_ORDER = 2
HYENA_EMB = 33
HYENA_DECAY_TARGET = 1e-2
HYENA_MIN_DECAY = math.log(HYENA_DECAY_TARGET) / 1.5
HYENA_MAX_DECAY = math.log(HYENA_DECAY_TARGET) / 0.3

N_CONV = 3 * GDN_WIDTH + 3 * HYENA_WIDTH
N_IN = N_CONV + GDN_WIDTH + 4 * GDN_HEADS
LANES = 128
N_IN_PAD = -(-N_IN // LANES) * LANES

DIFF_HEAD_DIM = 64
DIFF_HEADS = D_MODEL // (2 * DIFF_HEAD_DIM)
DIFF_V_DIM = 2 * DIFF_HEAD_DIM
DIFF_QK_WIDTH = DIFF_HEADS * 2 * DIFF_HEAD_DIM
DIFF_V_WIDTH = DIFF_HEADS * DIFF_V_DIM
ROPE_BASE = 10000.0
ROPE_FREQS = DIFF_HEAD_DIM // 4

PEER_HEADS = 8
PEER_KEYS = 128
PEER_TOPK = 16
PEER_QDIM = 256
PEER_TOKEN_BLOCK = 128
PEER_PICKS = PEER_HEADS * PEER_TOPK
N_EXPERTS = PEER_KEYS * PEER_KEYS
PEER_TOKENS_PER_STEP = 128
ROW_TILE = 8
NORM_EPS = 1e-6

VMEM_LIMIT = 56 * 1024 * 1024


def _cparams(*sem):
    return pltpu.CompilerParams(dimension_semantics=sem, vmem_limit_bytes=VMEM_LIMIT)


def _norm_mod(x, g, mod):
    y = x * lax.rsqrt(jnp.mean(x * x, axis=-1, keepdims=True) + NORM_EPS) * g
    return y * (1.0 + mod[1:2, :]) + mod[0:1, :]


def _proj_kernel(x_ref, g_ref, mod_ref, w_ref, o_ref):
    h = _norm_mod(x_ref[0], g_ref[...], mod_ref[0, 0])
    o_ref[0] = jnp.dot(h.astype(BF16), w_ref[...], preferred_element_type=F32).astype(o_ref.dtype)


def _proj_h_kernel(x_ref, g_ref, mod_ref, w_ref, h_ref, o_ref):
    h = _norm_mod(x_ref[0], g_ref[...], mod_ref[0, 0])
    h_ref[0] = h
    o_ref[0] = jnp.dot(h.astype(BF16), w_ref[...], preferred_element_type=F32)


def _proj_rope_kernel(x_ref, g_ref, mod_ref, w_ref, cos_ref, sin_ref, o_ref):
    h = _norm_mod(x_ref[0], g_ref[...], mod_ref[0, 0])
    p = jnp.dot(h.astype(BF16), w_ref[...], preferred_element_type=F32)
    cos = cos_ref[...]
    sin = sin_ref[...]
    lane = lax.broadcasted_iota(jnp.int32, (1, LANES), 1)
    first_half = (lane % (2 * ROPE_FREQS)) < ROPE_FREQS
    n_rot = 2 * DIFF_QK_WIDTH // LANES
    for j in range(p.shape[1] // LANES):
        blk = p[:, j * LANES:(j + 1) * LANES]
        if j < n_rot:
            partner = jnp.where(first_half, pltpu.roll(blk, LANES - ROPE_FREQS, 1),
                                pltpu.roll(blk, ROPE_FREQS, 1))
            blk = blk * cos + partner * sin
            if j < n_rot // 2:
                blk = blk * (DIFF_HEAD_DIM ** -0.5)
        o_ref[0, :, j * LANES:(j + 1) * LANES] = blk.astype(o_ref.dtype)


def _row_specs(n_out):
    x_spec = pl.BlockSpec((1, ROW_BLOCK, D_MODEL), lambda b, s: (b, s, 0))
    g_spec = pl.BlockSpec((1, D_MODEL), lambda b, s: (0, 0))
    mod_spec = pl.BlockSpec((1, 1, 2, D_MODEL), lambda b, s: (b, jnp.minimum(s, 1), 0, 0))
    w_spec = pl.BlockSpec((D_MODEL, n_out), lambda b, s: (0, 0))
    o_spec = pl.BlockSpec((1, ROW_BLOCK, n_out), lambda b, s: (b, s, 0))
    return x_spec, g_spec, mod_spec, w_spec, o_spec


def norm_mod_proj(x, g, mod, w, out_dtype=F32):
    n_out = w.shape[1]
    x_spec, g_spec, mod_spec, w_spec, o_spec = _row_specs(n_out)
    return pl.pallas_call(
        _proj_kernel, grid=(BATCH, N_ROW_BLOCKS),
        in_specs=[x_spec, g_spec, mod_spec, w_spec], out_specs=o_spec,
        out_shape=jax.ShapeDtypeStruct((BATCH, SEQ_ALL, n_out), out_dtype),
        compiler_params=_cparams("parallel", "parallel"), name="norm_mod_proj",
    )(x, g.reshape(1, D_MODEL), mod, w)


def norm_mod_proj_h(x, g, mod, w):
    n_out = w.shape[1]
    x_spec, g_spec, mod_spec, w_spec, o_spec = _row_specs(n_out)
    return pl.pallas_call(
        _proj_h_kernel, grid=(BATCH, N_ROW_BLOCKS),
        in_specs=[x_spec, g_spec, mod_spec, w_spec], out_specs=[x_spec, o_spec],
        out_shape=[jax.ShapeDtypeStruct((BATCH, SEQ_ALL, D_MODEL), F32),
                   jax.ShapeDtypeStruct((BATCH, SEQ_ALL, n_out), F32)],
        compiler_params=_cparams("parallel", "parallel"), name="norm_mod_proj_h",
    )(x, g.reshape(1, D_MODEL), mod, w)


def norm_mod_proj_rope(x, g, mod, w, cos_tab, sin_tab):
    n_out = w.shape[1]
    x_spec, g_spec, mod_spec, w_spec, o_spec = _row_specs(n_out)
    tab_spec = pl.BlockSpec((ROW_BLOCK, LANES), lambda b, s: (s, 0))
    return pl.pallas_call(
        _proj_rope_kernel, grid=(BATCH, N_ROW_BLOCKS),
        in_specs=[x_spec, g_spec, mod_spec, w_spec, tab_spec, tab_spec], out_specs=o_spec,
        out_shape=jax.ShapeDtypeStruct((BATCH, SEQ_ALL, n_out), BF16),
        compiler_params=_cparams("parallel", "parallel"), name="norm_mod_proj_rope",
    )(x, g.reshape(1, D_MODEL), mod, w, cos_tab, sin_tab)


def _out_proj_kernel(a_ref, w_ref, x_ref, gate_ref, o_ref):
    y = jnp.dot(a_ref[0].astype(BF16), w_ref[...], preferred_element_type=F32)
    o_ref[0] = x_ref[0] + gate_ref[0, 0] * y


def out_proj_residual(a, w, x, gate):
    k = a.shape[-1]
    return pl.pallas_call(
        _out_proj_kernel, grid=(BATCH, N_ROW_BLOCKS),
        in_specs=[pl.BlockSpec((1, ROW_BLOCK, k), lambda b, s: (b, s, 0)),
                  pl.BlockSpec((k, D_MODEL), lambda b, s: (0, 0)),
                  pl.BlockSpec((1, ROW_BLOCK, D_MODEL), lambda b, s: (b, s, 0)),
                  pl.BlockSpec((1, 1, 1, D_MODEL), lambda b, s: (b, jnp.minimum(s, 1), 0, 0))],
        out_specs=pl.BlockSpec((1, ROW_BLOCK, D_MODEL), lambda b, s: (b, s, 0)),
        out_shape=jax.ShapeDtypeStruct((BATCH, SEQ_ALL, D_MODEL), F32),
        compiler_params=_cparams("parallel", "parallel"), name="out_proj_residual",
    )(a, w, x, gate)


def _diff_attn_kernel(lam_ref, q_ref, k_ref, v_ref, gain_ref, o_ref, *, out_scale):
    lam = lam_ref[0]
    lane = lax.broadcasted_iota(jnp.int32, (1, LANES), 1)
    in_map0 = lane < DIFF_HEAD_DIM

    def attend(n_keys):
        q = q_ref[0]
        k = k_ref[0, :n_keys, :]
        v = v_ref[0, :n_keys, :]
        zero = jnp.zeros_like(q)
        nt = (((1,), (1,)), ((), ()))
        s0 = lax.dot_general(jnp.where(in_map0, q, zero), k, nt, preferred_element_type=F32)
        s1 = lax.dot_general(jnp.where(in_map0, zero, q), k, nt, preferred_element_type=F32)
        p0 = jnp.exp(s0 - jnp.max(s0, axis=-1, keepdims=True))
        p1 = jnp.exp(s1 - jnp.max(s1, axis=-1, keepdims=True))
        r0 = 1.0 / jnp.sum(p0, axis=-1, keepdims=True)
        r1 = lam / jnp.sum(p1, axis=-1, keepdims=True)
        a = p0 * r0 - p1 * r1
        o = jnp.dot(a.astype(BF16), v, preferred_element_type=F32)
        o = o * lax.rsqrt(jnp.mean(o * o, axis=-1, keepdims=True) + NORM_EPS)
        o_ref[0] = (o * gain_ref[...] * out_scale).astype(o_ref.dtype)

    is_ctx = pl.program_id(2) == 0

    @pl.when(is_ctx)
    def _():
        attend(CTX_LEN)

    @pl.when(jnp.logical_not(is_ctx))
    def _():
        attend(SEQ_ALL)


def diff_attention(qkv, lam, gain, out_scale):
    kv_blocks = DIFF_QK_WIDTH // LANES
    return pl.pallas_call(
        functools.partial(_diff_attn_kernel, out_scale=out_scale),
        grid=(BATCH, DIFF_HEADS, N_ROW_BLOCKS),
        in_specs=[pl.BlockSpec(memory_space=pltpu.SMEM),
                  pl.BlockSpec((1, ROW_BLOCK, LANES), lambda b, h, s: (b, s, h)),
                  pl.BlockSpec((1, SEQ_ALL, LANES), lambda b, h, s: (b, 0, kv_blocks + h)),
                  pl.BlockSpec((1, SEQ_ALL, LANES), lambda b, h, s: (b, 0, 2 * kv_blocks + h)),
                  pl.BlockSpec((1, LANES), lambda b, h, s: (0, 0))],
        out_specs=pl.BlockSpec((1, ROW_BLOCK, LANES), lambda b, h, s: (b, s, h)),
        out_shape=jax.ShapeDtypeStruct((BATCH, SEQ_ALL, DIFF_V_WIDTH), BF16),
        compiler_params=_cparams("parallel", "parallel", "arbitrary"), name="diff_attention",
    )(lam.reshape(1), qkv, qkv, qkv, gain.reshape(1, LANES))


def rope_tables():
    t = jnp.arange(SEQ)
    pos = jnp.stack([t // GRID_W, t % GRID_W], axis=-1).astype(F32)
    inv = ROPE_BASE ** (-jnp.arange(ROPE_FREQS, dtype=F32) / ROPE_FREQS)
    ang = pos[:, :, None] * inv
    cos = jnp.cos(ang)
    sin = jnp.sin(ang)
    cos_l = jnp.tile(jnp.concatenate([cos, cos], axis=-1).reshape(SEQ, 4 * ROPE_FREQS), (1, 2))
    sin_l = jnp.tile(jnp.concatenate([-sin, sin], axis=-1).reshape(SEQ, 4 * ROPE_FREQS), (1, 2))
    cos_t = jnp.concatenate([jnp.ones((CTX_LEN, LANES), F32), cos_l], axis=0)
    sin_t = jnp.concatenate([jnp.zeros((CTX_LEN, LANES), F32), sin_l], axis=0)
    return cos_t, sin_t


def _rmsnorm(x, g):
    xf = x.astype(F32)
    y = xf * lax.rsqrt(jnp.mean(xf * xf, axis=-1, keepdims=True) + NORM_EPS)
    return (y * g.astype(F32)).astype(x.dtype)


def _l2norm(x):
    x = x.astype(F32)
    return x * lax.rsqrt(jnp.sum(x * x, axis=-1, keepdims=True) + NORM_EPS)


def _depthwise_conv(x, w):
    return lax.conv_general_dilated(x, w[:, None, :].astype(x.dtype), window_strides=(1,),
                                    padding=[(SHORT_CONV // 2, SHORT_CONV // 2)],
                                    dimension_numbers=('NWC', 'WIO', 'NWC'),
                                    feature_group_count=x.shape[-1])


def _gated_delta_chunked(q, k, v, beta, g, s0, reverse=False):
    B, H, L, dk = q.shape
    dv = v.shape[-1]
    C = GDN_CHUNK
    n = L // C
    hp = lax.Precision.HIGHEST
    q, k, v, beta, g = [t.astype(F32).reshape(B, H, n, C, *t.shape[3:]) for t in (q, k, v, beta, g)]
    G = jnp.cumsum(g, axis=-1)
    causal = jnp.tril(jnp.ones((C, C), bool))
    strict = jnp.tril(jnp.ones((C, C), bool), -1)
    gdiff = G[..., :, None] - G[..., None, :]
    decay = jnp.where(causal, jnp.exp(jnp.where(causal, gdiff, 0.0)), 0.0)
    kk = jnp.einsum('bhncd,bhnjd->bhncj', k, k, precision=hp)
    a_mat = jnp.where(strict, beta[..., :, None] * kk * decay, 0.0)
    eye = jnp.eye(C, dtype=F32)
    rhs = jnp.concatenate([beta[..., None] * v, (beta * jnp.exp(G))[..., None] * k], axis=-1)
    sol = lax.linalg.triangular_solve(eye + a_mat, rhs, left_side=True, lower=True, unit_diagonal=True)
    w_intra, k_cum = sol[..., :dv], sol[..., dv:]
    qk = jnp.einsum('bhncd,bhnjd->bhncj', q, k, precision=hp) * decay
    q_dec = q * jnp.exp(G)[..., None]
    k_dec = k * jnp.exp(G[..., -1:] - G)[..., None]
    chunk_decay = jnp.exp(G[..., -1])

    def step(s, inp):
        w_i, kc_i, qk_i, qd_i, kd_i, cd_i = inp
        w = w_i - jnp.einsum('bhcd,bhde->bhce', kc_i, s, precision=hp)
        o = (jnp.einsum('bhcd,bhde->bhce', qd_i, s, precision=hp)
             + jnp.einsum('bhcj,bhje->bhce', qk_i, w, precision=hp))
        s = s * cd_i[..., None, None] + jnp.einsum('bhcd,bhce->bhde', kd_i, w, precision=hp)
        return s, o

    xs = tuple(jnp.moveaxis(t, 2, 0) for t in (w_intra, k_cum, qk, q_dec, k_dec, chunk_decay))
    s_fin, o = lax.scan(step, s0.astype(F32), xs, reverse=reverse)
    o = jnp.moveaxis(o, 0, 2).reshape(B, H, L, dv)
    return o, s_fin


def _gdn_inputs(qkv, ba, a_log, dt_bias):
    B, L, _ = qkv.shape
    qkv = jnp.transpose(jax.nn.silu(qkv).reshape(B, L, 3, GDN_HEADS, GDN_HEAD_DIM), (2, 0, 3, 1, 4))
    q = _l2norm(qkv[0]) * (GDN_HEAD_DIM ** -0.5)
    k = _l2norm(qkv[1])
    v = qkv[2]
    ba = ba.astype(F32).reshape(B, L, 2, 2, GDN_HEADS)
    beta = jax.nn.sigmoid(ba[:, :, 0])
    g = -jnp.exp(a_log.astype(F32)) * jax.nn.softplus(ba[:, :, 1] + dt_bias.astype(F32))
    return q, k, v, jnp.transpose(beta, (2, 0, 3, 1)), jnp.transpose(g, (2, 0, 3, 1))


def _gdn_bidir(q, k, v, beta, g, s0):
    o_f, s_f = _gated_delta_chunked(q, k, v, beta[0], g[0], s0[0])
    anti_eye = jnp.flip(jnp.eye(GDN_CHUNK, dtype=F32), axis=0)

    def fl_rows(t):
        B, H, L, D = t.shape
        t = t.reshape(B, H, L // GDN_CHUNK, GDN_CHUNK, D)
        return jnp.einsum('ij,bhnjd->bhnid', anti_eye, t, precision=lax.Precision.HIGHEST).reshape(B, H, L, D)

    def fl_scalars(t):
        B, H, L = t.shape
        return jnp.flip(t.reshape(B, H, L // GDN_CHUNK, GDN_CHUNK), axis=3).reshape(B, H, L)

    o_b, s_b = _gated_delta_chunked(fl_rows(q), fl_rows(k), fl_rows(v), fl_scalars(beta[1]), fl_scalars(g[1]),
                                    s0[1], reverse=True)
    return o_f + fl_rows(o_b), jnp.stack([s_f, s_b])


def _hyena_filters(L, w1, b1, w2, b2, w3, b3, w4, freq):
    hp = lax.Precision.HIGHEST
    t = jnp.linspace(0.0, 1.0, L, dtype=F32)
    bands = (HYENA_EMB - 1) // 2
    wpos = 2.0 * math.pi * jnp.arange(L, dtype=F32) / L
    fb = jnp.linspace(1e-4, bands - 1, bands, dtype=F32)
    z = jnp.concatenate([t[:, None], jnp.cos(wpos[:, None] * fb), -jnp.sin(wpos[:, None] * fb)], axis=-1)
    h = jnp.sin(freq * (jnp.dot(z, w1, precision=hp) + b1))
    h = jnp.sin(freq * (jnp.dot(h, w2, precision=hp) + b2))
    h = jnp.sin(freq * (jnp.dot(h, w3, precision=hp) + b3))
    h = jnp.dot(h, w4, precision=hp).astype(F32).reshape(L, HYENA_ORDER, 2, HYENA_WIDTH)
    deltas = jnp.linspace(HYENA_MIN_DECAY, HYENA_MAX_DECAY, HYENA_WIDTH, dtype=F32)
    window = jnp.exp(-t[:, None] * jnp.abs(deltas))
    return h * window[:, None, None, :]


def _two_sided_long_conv(u, h_fwd, h_bwd, bias):
    B, L, C = u.shape
    filt2 = jnp.concatenate([h_fwd, jnp.zeros((1, C), F32), jnp.flip(h_bwd[1:], axis=0)], axis=0)
    uf = jnp.fft.rfft(u.astype(F32), n=2 * L, axis=1)
    ff = jnp.fft.rfft(filt2, n=2 * L, axis=0)
    y = jnp.fft.irfft(uf * ff[None], n=2 * L, axis=1)[:, :L]
    return (y + u.astype(F32) * bias.astype(F32)).astype(u.dtype)


def _hyena(xh, filt, bias):
    z = xh[:, :, 2]
    for o in range(HYENA_ORDER):
        z = xh[:, :, o] * _two_sided_long_conv(z, filt[:, o, 0], filt[:, o, 1], bias[o])
    return z


def _even_mixer_core(p, conv_w, a_log, dt_bias, gdn_gain, hf, hy_bias):
    def split(pp):
        B, L, _ = pp.shape
        cv = _depthwise_conv(pp[..., :N_CONV], conv_w)
        return (cv[..., :3 * GDN_WIDTH], cv[..., 3 * GDN_WIDTH:].reshape(B, L, 3, HYENA_WIDTH),
                pp[..., N_CONV:N_CONV + GDN_WIDTH], pp[..., N_CONV + GDN_WIDTH:])

    def gdn_out(o, z):
        B, H, L, dv = o.shape
        o = jnp.transpose(o, (0, 2, 1, 3)).astype(z.dtype)
        o = _rmsnorm(o, gdn_gain) * jax.nn.silu(z.reshape(B, L, H, dv))
        return o.reshape(B, L, H * dv)

    def hyena_branch(xh):
        return _hyena(xh, _hyena_filters(xh.shape[1], *hf), hy_bias)

    B = p.shape[0]
    qkv_c, hy_c, z_c, ba_c = split(p[:, :CTX_LEN])
    q, k, v, beta, g = _gdn_inputs(qkv_c, ba_c, a_log, dt_bias)
    s0 = jnp.zeros((2, B, GDN_HEADS, GDN_HEAD_DIM, GDN_HEAD_DIM), F32)
    o_c, s_c = _gdn_bidir(q, k, v, beta, g, s0)
    qkv_l, hy_l, z_l, ba_l = split(p[:, CTX_LEN:])
    q, k, v, beta, g = _gdn_inputs(qkv_l, ba_l, a_log, dt_bias)
    o_l, _ = _gdn_bidir(q, k, v, beta, g, s_c)
    a_lat = jnp.concatenate([gdn_out(o_l, z_l), hyena_branch(hy_l)], axis=-1)
    a_ctx = jnp.concatenate([gdn_out(o_c, z_c), hyena_branch(hy_c)], axis=-1)
    return jnp.concatenate([a_ctx, a_lat], axis=1)


SEL_GROUPS = 2 * PEER_HEADS
SEL_CANDS = -(-sum(PEER_TOPK // (a + 1) for a in range(PEER_TOPK)) // ROW_TILE) * ROW_TILE
_HIGHEST = lax.Precision.HIGHEST
_NT = (((1,), (1,)), ((), ()))


def _peer_select_kernel(q_ref, keys_ref, idx_ref, gate_ref, s_scr, sv_scr, si_scr, cand_scr, cidx_scr, val_scr):
    n_tok = PEER_TOKENS_PER_STEP
    row_f = lax.broadcasted_iota(jnp.int32, (PEER_KEYS, n_tok), 0).astype(F32)
    row2_f = lax.broadcasted_iota(jnp.int32, (SEL_CANDS, n_tok), 0).astype(F32)
    neg_inf = jnp.float32(-jnp.inf)

    for g in range(SEL_GROUPS):
        qg = q_ref[:, g * LANES:(g + 1) * LANES]
        s_scr[g] = lax.dot_general(keys_ref[g], qg, _NT, precision=_HIGHEST, preferred_element_type=F32)

    def key_pick(k, c):
        for g in range(SEL_GROUPS):
            s = s_scr[g]
            m = jnp.max(s, axis=0, keepdims=True)
            pos = jnp.min(jnp.where(s == m, row_f, float(PEER_KEYS)), axis=0, keepdims=True)
            s_scr[g] = jnp.where(row_f == pos, neg_inf, s)
            sv_scr[g, pl.ds(k, 1), :] = m
            si_scr[g, pl.ds(k, 1), :] = pos
        return c

    lax.fori_loop(0, PEER_TOPK, key_pick, 0)

    for h in range(PEER_HEADS):
        sv0, sv1 = sv_scr[2 * h], sv_scr[2 * h + 1]
        si0, si1 = si_scr[2 * h], si_scr[2 * h + 1]
        cand_scr[h] = jnp.full((SEL_CANDS, n_tok), neg_inf, F32)
        cidx_scr[h] = jnp.zeros((SEL_CANDS, n_tok), F32)
        r0 = 0
        for a in range(PEER_TOPK):
            nb = PEER_TOPK // (a + 1)
            cand_scr[h, r0:r0 + nb, :] = sv0[a:a + 1, :] + sv1[0:nb, :]
            cidx_scr[h, r0:r0 + nb, :] = si0[a:a + 1, :] * float(PEER_KEYS) + si1[0:nb, :]
            r0 += nb

    def expert_pick(k, c):
        for h in range(PEER_HEADS):
            cd = cand_scr[h]
            m = jnp.max(cd, axis=0, keepdims=True)
            pos = jnp.min(jnp.where(cd == m, row2_f, float(2 * SEL_CANDS)), axis=0, keepdims=True)
            hit = row2_f == pos
            e = jnp.max(jnp.where(hit, cidx_scr[h], -1.0), axis=0, keepdims=True)
            cand_scr[h] = jnp.where(hit, neg_inf, cd)
            val_scr[pl.ds(h * PEER_TOPK + k, 1), :] = m
            idx_ref[0, pl.ds(h * PEER_TOPK + k, 1), :] = e.astype(jnp.int32)
        return c

    lax.fori_loop(0, PEER_TOPK, expert_pick, 0)

    for h in range(PEER_HEADS):
        v = val_scr[h * PEER_TOPK:(h + 1) * PEER_TOPK, :]
        ex = jnp.exp(v - v[0:1, :])
        gate_ref[0, h * PEER_TOPK:(h + 1) * PEER_TOPK, :] = ex / jnp.sum(ex, axis=0, keepdims=True)


def peer_select(q, keys):
    n_steps = q.shape[0] // PEER_TOKENS_PER_STEP
    out_spec = pl.BlockSpec((1, PEER_PICKS, PEER_TOKENS_PER_STEP), lambda i: (i, 0, 0))
    scr = lambda n, r: pltpu.VMEM((n, r, PEER_TOKENS_PER_STEP), F32)
    return pl.pallas_call(
        _peer_select_kernel, grid=(n_steps,),
        in_specs=[pl.BlockSpec((PEER_TOKENS_PER_STEP, PEER_HEADS * PEER_QDIM), lambda i: (i, 0)),
                  pl.BlockSpec((SEL_GROUPS, PEER_KEYS, PEER_QDIM // 2), lambda i: (0, 0, 0))],
        out_specs=[out_spec, out_spec],
        out_shape=[jax.ShapeDtypeStruct((n_steps, PEER_PICKS, PEER_TOKENS_PER_STEP), jnp.int32),
                   jax.ShapeDtypeStruct((n_steps, PEER_PICKS, PEER_TOKENS_PER_STEP), F32)],
        scratch_shapes=[scr(SEL_GROUPS, PEER_KEYS), scr(SEL_GROUPS, PEER_TOPK), scr(SEL_GROUPS, PEER_TOPK),
                        scr(PEER_HEADS, SEL_CANDS), scr(PEER_HEADS, SEL_CANDS),
                        pltpu.VMEM((PEER_PICKS, PEER_TOKENS_PER_STEP), F32)],
        compiler_params=_cparams("parallel"), name="peer_select",
    )(q, keys.reshape(SEL_GROUPS, PEER_KEYS, PEER_QDIM // 2))


def pack_expert_table(tab):
    bits = lax.bitcast_convert_type(tab.astype(BF16), jnp.uint16).astype(jnp.uint32)
    bits = bits.reshape(N_EXPERTS // 2, 2, D_MODEL // LANES, LANES)
    return (bits[:, 0] | (bits[:, 1] << 16)).reshape(N_EXPERTS // 2 * ROW_TILE, LANES)


def _expert_row(tab_ref, offset, shift):
    w = tab_ref[pl.ds(pl.multiple_of(offset, ROW_TILE), ROW_TILE), :]
    return lax.bitcast_convert_type((w << shift.astype(jnp.uint32)) & jnp.uint32(0xFFFF0000), F32)


def _peer_act_kernel(off_ref, shift_ref, h_ref, tab_ref, act_ref):
    lane = lax.broadcasted_iota(jnp.int32, (ROW_TILE, LANES), 1)

    def token(t, carry):
        h = h_ref[t]
        acc = jnp.zeros((ROW_TILE, LANES), F32)
        for p in range(PEER_PICKS):
            s = p * PEER_TOKENS_PER_STEP + t
            row = _expert_row(tab_ref, off_ref[s], shift_ref[s])
            acc = jnp.where(lane == p, jnp.sum(row * h, axis=1, keepdims=True), acc)
        act_ref[pl.ds(t, 1), :] = jnp.sum(acc, axis=0, keepdims=True)
        return carry

    lax.fori_loop(0, PEER_TOKENS_PER_STEP, token, 0, unroll=8)


OUT_GROUP = 8
OUT_GROUPS = PEER_PICKS // OUT_GROUP
PAIR_ROWS = 2 * ROW_TILE


def pack_expert_rows(tab):
    return tab.astype(BF16).reshape(N_EXPERTS // 2, PAIR_ROWS, LANES)


def _peer_out_kernel(tile_ref, coef_ref, par_ref, expand_ref, x_ref, gate_ref, tab_ref, o_ref,
                     hi_scr, lo_scr, pe_scr):
    coef = coef_ref[...]
    c_hi = coef.astype(BF16)
    c_lo = (coef - c_hi.astype(F32)).astype(BF16)
    spread = expand_ref[...]
    hi_scr[...] = jnp.dot(c_hi, spread, preferred_element_type=F32)
    lo_scr[...] = jnp.dot(c_lo, spread, preferred_element_type=F32)
    pe_scr[...] = jnp.dot(par_ref[...].astype(BF16), spread, preferred_element_type=F32)
    lane = lax.broadcasted_iota(jnp.int32, (ROW_TILE, LANES), 1)
    sub = lax.broadcasted_iota(jnp.int32, (ROW_TILE, LANES), 0)
    seg_hit = (lane % ROW_TILE) == sub
    lane_par = ((lane % PAIR_ROWS) // ROW_TILE).astype(F32)

    def octet(o, carry):
        t0 = pl.multiple_of(o * ROW_TILE, ROW_TILE)
        hi8 = hi_scr[pl.ds(t0, ROW_TILE), :]
        lo8 = lo_scr[pl.ds(t0, ROW_TILE), :]
        pe8 = pe_scr[pl.ds(t0, ROW_TILE), :]
        for r in range(ROW_TILE):
            t = t0 + r
            acc = jnp.zeros((PAIR_ROWS, LANES), F32)
            for g in range(OUT_GROUPS):
                tiles = [tab_ref[tile_ref[(g * OUT_GROUP + q) * PEER_TOKENS_PER_STEP + t]] for q in range(OUT_GROUP)]
                rhs = jnp.concatenate(tiles, axis=0)
                cols = slice(g * LANES, (g + 1) * LANES)
                hit = seg_hit & (lane_par == pe8[r:r + 1, cols])
                hi = jnp.where(hit, hi8[r:r + 1, cols], 0.0).astype(BF16)
                lo = jnp.where(hit, lo8[r:r + 1, cols], 0.0).astype(BF16)
                acc = acc + jnp.dot(jnp.concatenate([hi, lo], axis=0), rhs, preferred_element_type=F32)
            o_ref[t] = x_ref[t] + gate_ref[0, 0] * (acc[:ROW_TILE] + acc[ROW_TILE:])
        return carry

    lax.fori_loop(0, PEER_TOKENS_PER_STEP // ROW_TILE, octet, 0)


def _peer_specs():
    n_tok = BATCH * SEQ_ALL
    steps_per_batch = SEQ_ALL // PEER_TOKENS_PER_STEP
    ctx_steps = CTX_LEN // PEER_TOKENS_PER_STEP
    flat = PEER_TOKENS_PER_STEP * PEER_PICKS
    smem = pl.BlockSpec((flat,), lambda i: (i,), memory_space=pltpu.SMEM)
    tok = pl.BlockSpec((PEER_TOKENS_PER_STEP, ROW_TILE, LANES), lambda i: (i, 0, 0))
    tab = pl.BlockSpec((N_EXPERTS // 2 * ROW_TILE, LANES), lambda i: (0, 0), pipeline_mode=pl.Buffered(1))
    gate = pl.BlockSpec((1, 1, ROW_TILE, LANES),
                        lambda i: (i // steps_per_batch, jnp.minimum((i % steps_per_batch) // ctx_steps, 1), 0, 0))
    picks = pl.BlockSpec((PEER_TOKENS_PER_STEP, PEER_PICKS), lambda i: (i, 0))
    return n_tok, smem, tok, tab, gate, picks


def peer_activations(offset, shift, h, tab):
    n_tok, smem, tok, tabspec, _, picks = _peer_specs()
    return pl.pallas_call(
        _peer_act_kernel, grid=(n_tok // PEER_TOKENS_PER_STEP,),
        in_specs=[smem, smem, tok, tabspec], out_specs=picks,
        out_shape=jax.ShapeDtypeStruct((n_tok, PEER_PICKS), F32),
        compiler_params=_cparams("parallel"), name="peer_activations",
    )(offset, shift, h.reshape(n_tok, ROW_TILE, LANES), tab)


def peer_output_residual(tile, coef, parity, x, gate, tab):
    n_tok, smem, tok, _, gatespec, picks = _peer_specs()
    p = jnp.arange(PEER_PICKS)
    col = jnp.arange(OUT_GROUPS * LANES)
    expand = (p[:, None] == (col[None, :] // LANES) * OUT_GROUP + (col[None, :] % LANES) // PAIR_ROWS).astype(BF16)
    wide = OUT_GROUPS * LANES
    spread_scratch = pltpu.VMEM((PEER_TOKENS_PER_STEP, wide), F32)
    out = pl.pallas_call(
        _peer_out_kernel, grid=(n_tok // PEER_TOKENS_PER_STEP,),
        in_specs=[smem, picks, picks, pl.BlockSpec((PEER_PICKS, wide), lambda i: (0, 0)), tok, gatespec,
                  pl.BlockSpec((N_EXPERTS // 2, PAIR_ROWS, LANES), lambda i: (0, 0, 0), pipeline_mode=pl.Buffered(1))],
        out_specs=tok,
        out_shape=jax.ShapeDtypeStruct((n_tok, ROW_TILE, LANES), F32),
        scratch_shapes=[spread_scratch, spread_scratch, spread_scratch],
        compiler_params=_cparams("parallel"), name="peer_output_residual",
    )(tile, coef, parity, expand, x.reshape(n_tok, ROW_TILE, LANES), gate.reshape(BATCH, 2, ROW_TILE, LANES), tab)
    return out.reshape(BATCH, SEQ_ALL, D_MODEL)


def kernel(x, c, ctx, c_ctx, norm1, norm2, w_ada, b_ada, w_in, conv_w, a_log, dt_bias, gdn_gain, hf_w1, hf_b1, hf_w2, hf_b2, hf_w3, hf_b3, hf_w4, hf_freq, hy_bias, w_out_even, w_qkv, lam_q1, lam_k1, lam_q2, lam_k2, subln, w_out_odd, peer_wq, peer_keys, peer_u, peer_v, final_norm):
    hp = lax.Precision.HIGHEST
    xs = jnp.concatenate([ctx, x], axis=1)
    cos_tab, sin_tab = rope_tables()
    cond = jnp.concatenate([jax.nn.silu(c_ctx)[None], jax.nn.silu(c)], axis=0)
    for layer in range(DEPTH):
        m = (jnp.dot(cond, w_ada[layer], precision=hp) + b_ada[layer]).reshape(BATCH + 1, 6, D_MODEL)
        m = jnp.stack([jnp.broadcast_to(m[0], (BATCH, 6, D_MODEL)), m[1:]], axis=1)
        mod1, gate1 = m[:, :, 0:2], m[:, :, 2:3]
        mod2, gate2 = m[:, :, 3:5], m[:, :, 5:6]
        if layer % 2 == 0:
            e = layer // 2
            w_pad = jnp.pad(w_in[e], ((0, 0), (0, N_IN_PAD - N_IN))).astype(BF16)
            p = norm_mod_proj(xs, norm1[layer], mod1, w_pad)[..., :N_IN]
            hf = (hf_w1[e], hf_b1[e], hf_w2[e], hf_b2[e], hf_w3[e], hf_b3[e], hf_w4[e], hf_freq[e])
            a = _even_mixer_core(p, conv_w[e], a_log[e], dt_bias[e], gdn_gain[e], hf, hy_bias[e])
            xs = out_proj_residual(a, w_out_even[e].astype(BF16), xs, gate1)
        else:
            o = layer // 2
            lam_init = 0.8 - 0.6 * math.exp(-0.3 * layer)
            lam = (jnp.exp(jnp.sum(lam_q1[o].astype(F32) * lam_k1[o].astype(F32)))
                   - jnp.exp(jnp.sum(lam_q2[o].astype(F32) * lam_k2[o].astype(F32))) + lam_init)
            qkv = norm_mod_proj_rope(xs, norm1[layer], mod1, w_qkv[o].astype(BF16), cos_tab, sin_tab)
            a = diff_attention(qkv, lam, subln[o], 1.0 - lam_init)
            xs = out_proj_residual(a, w_out_odd[o].astype(BF16), xs, gate1)
        h2, q = norm_mod_proj_h(xs, norm2[layer], mod2, peer_wq[layer].astype(BF16))
        idx, gate = peer_select(q.reshape(BATCH * SEQ_ALL, -1), peer_keys[layer])
        tile = idx >> 1
        offset = (tile * ROW_TILE).reshape(-1)
        shift = (16 - 16 * (idx & 1)).reshape(-1)
        act = peer_activations(offset, shift, h2, pack_expert_table(peer_u[layer]))
        by_token = lambda a: jnp.swapaxes(a, 1, 2).reshape(-1, PEER_PICKS)
        coef = by_token(gate) * jax.nn.gelu(act, approximate=False)
        xs = peer_output_residual(tile.reshape(-1), coef, by_token(idx & 1).astype(F32), xs, gate2,
                                  pack_expert_rows(peer_v[layer]))
    return _rmsnorm(xs[:, CTX_LEN:], final_norm)
```
